```python
import math
import jax
import jax.numpy as jnp
from jax import lax
import numpy as np

D_MODEL = 1024
BATCH = 2
SEQ = 16384
DEPTH = 1
DEC_BATCH = 1
DEC_SEQ = 16384
PAST_LEN = 128

GRID_W = 64
N_MEM = 256
HEAD_DIM = 128
EPS = 1e-6
GDN_HEADS = 4
GDN_DK = 128
GDN_DV = 128
GDN_CONV = 5
GDN_CHUNK = 64
ATT_HEADS = 8
ATT_KV_HEADS = 2
Q_BLOCK = 128
ROPE_THETA = 10000.0
MEM_HEADS = 4
N_BRANCH = 3

W_A = GDN_HEADS * GDN_DV
W_B = ATT_HEADS * HEAD_DIM
W_C = MEM_HEADS * HEAD_DIM
N_QKV_A = 2 * GDN_HEADS * GDN_DK + GDN_HEADS * GDN_DV
N_SCAL_A = 4 * GDN_HEADS
N_KV_B = 2 * ATT_KV_HEADS * HEAD_DIM
N_GATE = N_BRANCH * D_MODEL
IN_SPLITS = (N_QKV_A, W_A, N_SCAL_A, W_B, N_KV_B, W_B, W_C, W_C, N_GATE)
N_IN = N_QKV_A + W_A + N_SCAL_A + W_B + N_KV_B + W_B + W_C + W_C + N_GATE

kernel_name = 'bidir_hybrid_gdn_axial_gqa_mem_encoder'


def rms_norm(x, g):
    xf = x.astype(jnp.float32)
    y = xf * lax.rsqrt(jnp.mean(xf * xf, axis=-1, keepdims=True) + EPS)
    return (y * g.astype(jnp.float32)).astype(x.dtype)


def l2_norm(x):
    return x * lax.rsqrt(jnp.sum(x * x, axis=-1, keepdims=True) + EPS)


def split_cols(x, sizes):
    out = []
    off = 0
    for s in sizes:
        out.append(x[..., off:off + s])
        off += s
    return out


def centred_depthwise_conv(x, w):
    c = x.shape[-1]
    pad = (GDN_CONV - 1) // 2
    return lax.conv_general_dilated(x, w[:, None, :].astype(x.dtype), window_strides=(1,),
                                    padding=[(pad, pad)], dimension_numbers=('NWC', 'WIO', 'NWC'),
                                    feature_group_count=c)


def gated_delta_chunked(q, k, v, beta, g):
    bsz, nh, seq, dk = k.shape
    dv = v.shape[-1]
    c = GDN_CHUNK
    n = seq // c
    q = q.reshape(bsz, nh, n, c, dk)
    k = k.reshape(bsz, nh, n, c, dk)
    v = v.reshape(bsz, nh, n, c, dv)
    beta = beta.reshape(bsz, nh, n, c)
    G = jnp.cumsum(g.reshape(bsz, nh, n, c), axis=-1)
    causal = jnp.tril(jnp.ones((c, c), dtype=bool))
    strict = jnp.tril(jnp.ones((c, c), dtype=bool), -1)
    diff = G[..., :, None] - G[..., None, :]
    decay = jnp.where(causal, jnp.exp(jnp.where(causal, diff, 0.0)), 0.0)
    k_beta = k * beta[..., None]
    A = jnp.where(strict, jnp.einsum('bhnid,bhnjd->bhnij', k_beta, k) * decay, 0.0)
    eye = jnp.eye(c, dtype=k.dtype)
    T = lax.linalg.triangular_solve(eye + A, jnp.broadcast_to(eye, A.shape), left_side=True, lower=True)
    u_v = jnp.einsum('bhnij,bhnjd->bhnid', T, v * beta[..., None])
    w_k = jnp.einsum('bhnij,bhnjd->bhnid', T, k_beta * jnp.exp(G)[..., None])
    qk = jnp.where(causal, jnp.einsum('bhnid,bhnjd->bhnij', q, k) * decay, 0.0)
    q_dec = q * jnp.exp(G)[..., None]
    k_dec = k * jnp.exp(G[..., -1:] - G)[..., None]
    g_last = jnp.exp(G[..., -1])

    def step(state, xs):
        u_v_n, w_k_n, qk_n, q_dec_n, k_dec_n, gl_n = xs
        u = u_v_n - jnp.einsum('bhcd,bhde->bhce', w_k_n, state)
        o = jnp.einsum('bhcd,bhde->bhce', q_dec_n, state) + jnp.einsum('bhij,bhje->bhie', qk_n, u)
        new_state = state * gl_n[..., None, None] + jnp.einsum('bhcd,bhce->bhde', k_dec_n, u)
        return new_state, o

    xs = tuple(jnp.moveaxis(t, 2, 0) for t in (u_v, w_k, qk, q_dec, k_dec, g_last))
    _, o = lax.scan(step, jnp.zeros((bsz, nh, dk, dv), jnp.float32), xs)
    return jnp.moveaxis(o, 0, 2).reshape(bsz, nh, seq, dv)


def gdn_branch(qkv, scal, w_conv, a_log_f, a_log_b, dt_bias_f, dt_bias_b):
    bsz, seq, _ = qkv.shape
    qkv = jax.nn.silu(centred_depthwise_conv(qkv, w_conv)).astype(jnp.float32)
    nqk = GDN_HEADS * GDN_DK
    q = qkv[..., :nqk].reshape(bsz, seq, GDN_HEADS, GDN_DK)
    k = qkv[..., nqk:2 * nqk].reshape(bsz, seq, GDN_HEADS, GDN_DK)
    v = qkv[..., 2 * nqk:].reshape(bsz, seq, GDN_HEADS, GDN_DV)
    q = l2_norm(q) * (GDN_DK ** -0.5)
    k = l2_norm(k)
    q, k, v = (t.transpose(0, 2, 1, 3) for t in (q, k, v))
    scal = scal.astype(jnp.float32).reshape(bsz, seq, 4, GDN_HEADS).transpose(2, 0, 3, 1)
    b_f, b_b, a_f, a_b = scal[0], scal[1], scal[2], scal[3]

    def log_decay(a, a_log, dt_bias):
        return -jnp.exp(a_log.astype(jnp.float32))[:, None] * jax.nn.softplus(a + dt_bias.astype(jnp.float32)[:, None])

    def flip(t):
        return jnp.flip(t, axis=2)

    o_f = gated_delta_chunked(q, k, v, jax.nn.sigmoid(b_f), log_decay(a_f, a_log_f, dt_bias_f))
    o_b = flip(gated_delta_chunked(flip(q), flip(k), flip(v), flip(jax.nn.sigmoid(b_b)),
                                   flip(log_decay(a_b, a_log_b, dt_bias_b))))
    return (o_f + o_b).transpose(0, 2, 1, 3)


def rope_half(x, pos):
    d = x.shape[-1]
    freqs = ROPE_THETA ** (-jnp.arange(0, d, 2, dtype=jnp.float32) / d)
    ang = pos.astype(jnp.float32)[:, None] * freqs[None, :]
    cos = jnp.cos(ang)[None, :, None, :]
    sin = jnp.sin(ang)[None, :, None, :]
    x1 = x[..., :d // 2]
    x2 = x[..., d // 2:]
    return jnp.concatenate([x1 * cos - x2 * sin, x2 * cos + x1 * sin], axis=-1)


def axial_rope(x, row, col):
    xf = x.astype(jnp.float32)
    half = x.shape[-1] // 2
    y = jnp.concatenate([rope_half(xf[..., :half], row), rope_half(xf[..., half:], col)], axis=-1)
    return y.astype(x.dtype)


def grid_attention(q, k, v):
    bsz, seq, nh, hd = q.shape
    nkv = k.shape[2]
    grp = nh // nkv
    nb = seq // Q_BLOCK
    scale = hd ** -0.5
    qb = q.reshape(bsz, nb, Q_BLOCK, nkv, grp, hd).transpose(1, 0, 2, 3, 4, 5)

    def block(qblk):
        s = jnp.einsum('bqkgd,bskd->bkgqs', qblk, k, preferred_element_type=jnp.float32) * scale
        p = jax.nn.softmax(s, axis=-1)
        return jnp.einsum('bkgqs,bskd->bqkgd', p.astype(v.dtype), v)

    o = lax.map(block, qb)
    return o.transpose(1, 0, 2, 3, 4, 5).reshape(bsz, seq, nh * hd)


def memory_attention(q, k, v):
    bsz, seq, nh, hd = q.shape
    s = jnp.einsum('bshd,bmhd->bhsm', q, k, preferred_element_type=jnp.float32) * (hd ** -0.5)
    p = jax.nn.softmax(s, axis=-1)
    return jnp.einsum('bhsm,bmhd->bshd', p.astype(v.dtype), v).reshape(bsz, seq, nh * hd)


def encoder_layer(x, mem, g_norm, g_mem, w_in, w_conv, a_log_f, a_log_b, dt_bias_f, dt_bias_b,
                  g_gdn_out, g_q_attn, g_k_attn, g_q_mem, g_k_mem, w_mem_kv, w_down_a, w_down_b,
                  w_down_c, w_out):
    bsz, seq, _ = x.shape
    n_rows = seq // GRID_W
    row = jnp.repeat(jnp.arange(n_rows, dtype=jnp.int32), GRID_W)
    col = jnp.tile(jnp.arange(GRID_W, dtype=jnp.int32), n_rows)

    h = rms_norm(x, g_norm)
    proj = h @ w_in
    qkv_a, z_a, scal_a, q_b, kv_b, z_b, q_c, z_c, gate_logits = split_cols(proj, IN_SPLITS)

    o_a = gdn_branch(qkv_a, scal_a, w_conv, a_log_f, a_log_b, dt_bias_f, dt_bias_b)
    o_a = rms_norm(o_a, g_gdn_out).reshape(bsz, seq, W_A).astype(x.dtype) * jax.nn.silu(z_a)

    qh = q_b.reshape(bsz, seq, ATT_HEADS, HEAD_DIM)
    kh = kv_b[..., :N_KV_B // 2].reshape(bsz, seq, ATT_KV_HEADS, HEAD_DIM)
    vh = kv_b[..., N_KV_B // 2:].reshape(bsz, seq, ATT_KV_HEADS, HEAD_DIM)
    qh = axial_rope(rms_norm(qh, g_q_attn), row, col)
    kh = axial_rope(rms_norm(kh, g_k_attn), row, col)
    o_b = grid_attention(qh, kh, vh) * jax.nn.silu(z_b)

    mem_n = rms_norm(mem, g_mem)
    kv_m = mem_n @ w_mem_kv
    n_mem = mem.shape[1]
    k_m = rms_norm(kv_m[..., :W_C].reshape(bsz, n_mem, MEM_HEADS, HEAD_DIM), g_k_mem)
    v_m = kv_m[..., W_C:].reshape(bsz, n_mem, MEM_HEADS, HEAD_DIM)
    q_m = rms_norm(q_c.reshape(bsz, seq, MEM_HEADS, HEAD_DIM), g_q_mem)
    o_c = memory_attention(q_m, k_m, v_m) * jax.nn.silu(z_c)

    gates = jax.nn.sigmoid(gate_logits.astype(jnp.float32)).reshape(bsz, seq, N_BRANCH, D_MODEL)
    mixed = (gates[..., 0, :] * (o_a @ w_down_a) + gates[..., 1, :] * (o_b @ w_down_b)
             + gates[..., 2, :] * (o_c @ w_down_c))
    return x + (mixed.astype(x.dtype) @ w_out).astype(x.dtype)


def setup_inputs(seed: int = 0) -> dict:
    key = jax.random.key(seed)
    ks = jax.random.split(key, 24)
    f32 = jnp.float32
    L = DEPTH

    def nrm(k, shape, fan_in):
        return jax.random.normal(k, shape, f32) * (fan_in ** -0.5)

    def gain(k, shape):
        return 1.0 + 0.02 * jax.random.normal(k, shape, f32)

    dt = jnp.exp(jax.random.uniform(ks[8], (L, 2, GDN_HEADS), f32, math.log(1e-3), math.log(1e-1)))
    dt_bias = dt + jnp.log(-jnp.expm1(-dt))
    a_log = jnp.log(jax.random.uniform(ks[9], (L, 2, GDN_HEADS), f32, 1.0, 16.0))
    return {
        'x_prompt': jax.random.normal(ks[0], (BATCH, SEQ, D_MODEL), f32),
        'x_sample': jax.random.normal(ks[1], (DEC_BATCH, DEC_SEQ, D_MODEL), f32),
        'mem_prompt': jax.random.normal(ks[2], (BATCH, N_MEM, D_MODEL), f32),
        'mem_sample': jax.random.normal(ks[3], (DEC_BATCH, N_MEM, D_MODEL), f32),
        'g_norm': gain(ks[4], (L, D_MODEL)),
        'g_mem': gain(ks[5], (L, D_MODEL)),
        'w_in': nrm(ks[6], (L, D_MODEL, N_IN), D_MODEL),
        'w_conv': nrm(ks[7], (L, GDN_CONV, N_QKV_A), GDN_CONV),
        'a_log_f': a_log[:, 0],
        'a_log_b': a_log[:, 1],
        'dt_bias_f': dt_bias[:, 0],
        'dt_bias_b': dt_bias[:, 1],
        'g_gdn_out': gain(ks[10], (L, GDN_DV)),
        'g_q_attn': gain(ks[11], (L, HEAD_DIM)),
        'g_k_attn': gain(ks[12], (L, HEAD_DIM)),
        'g_q_mem': gain(ks[13], (L, HEAD_DIM)),
        'g_k_mem': gain(ks[14], (L, HEAD_DIM)),
        'w_mem_kv': nrm(ks[15], (L, D_MODEL, 2 * W_C), D_MODEL),
        'w_down_a': nrm(ks[16], (L, W_A, D_MODEL), W_A),
        'w_down_b': nrm(ks[17], (L, W_B, D_MODEL), W_B),
        'w_down_c': nrm(ks[18], (L, W_C, D_MODEL), W_C),
        'w_out': nrm(ks[19], (L, D_MODEL, D_MODEL), D_MODEL),
    }


def reference(x_prompt, x_sample, mem_prompt, mem_sample, g_norm, g_mem, w_in, w_conv, a_log_f,
              a_log_b, dt_bias_f, dt_bias_b, g_gdn_out, g_q_attn, g_k_attn, g_q_mem, g_k_mem,
              w_mem_kv, w_down_a, w_down_b, w_down_c, w_out):
    def trunk(x, mem):
        for l in range(DEPTH):
            x = encoder_layer(x, mem, g_norm[l], g_mem[l], w_in[l], w_conv[l], a_log_f[l], a_log_b[l],
                              dt_bias_f[l], dt_bias_b[l], g_gdn_out[l], g_q_attn[l], g_k_attn[l],
                              g_q_mem[l], g_k_mem[l], w_mem_kv[l], w_down_a[l], w_down_b[l],
                              w_down_c[l], w_out[l])
        return x

    y_prompt = trunk(x_prompt, mem_prompt)
    y_sample = trunk(x_sample, mem_sample)
    return (y_prompt, y_sample)
```

```python
import functools
import math

import jax
import jax.numpy as jnp
from jax import lax
from jax.experimental import pallas as pl
from jax.experimental.pallas import tpu as pltpu

F32 = jnp.float32
BF16 = jnp.bfloat16

D_MODEL = 1024
GRID_W = 64
HEAD_DIM = 128
EPS = 1e-6
GDN_HEADS = 4
GDN_DK = 128
GDN_DV = 128
GDN_CONV = 5
GDN_CHUNK = 64
ATT_HEADS = 8
ATT_KV_HEADS = 2
ROPE_THETA = 10000.0
MEM_HEADS = 4
N_BRANCH = 3

W_A = GDN_HEADS * GDN_DV
W_B = ATT_HEADS * HEAD_DIM
W_C = MEM_HEADS * HEAD_DIM
N_QKV_A = 2 * GDN_HEADS * GDN_DK + GDN_HEADS * GDN_DV
N_SCAL_A = 4 * GDN_HEADS
N_KV_B = 2 * ATT_KV_HEADS * HEAD_DIM
N_GATE = N_BRANCH * D_MODEL

LANES = 128
LOG2E = math.log2(math.e)
SM_SCALE = HEAD_DIM ** -0.5 * LOG2E

OFF_QKVA = 0
OFF_ZA = OFF_QKVA + N_QKV_A
OFF_QB = OFF_ZA + W_A
OFF_KVB = OFF_QB + W_B
OFF_ZB = OFF_KVB + N_KV_B
OFF_QC = OFF_ZB + W_B
OFF_ZC = OFF_QC + W_C
OFF_GATE = OFF_ZC + W_C
OFF_SCAL = OFF_GATE + N_GATE
N_PACK = OFF_SCAL + LANES

TOK_BLOCK = 512
GDN_BLOCK = 256
ATT_TQ = 256
ATT_TK = 512
VMEM_LIMIT = 56 * 1024 * 1024


def _sigmoid(x):
    return 1.0 / (1.0 + jnp.exp(-x))


def _softplus(x):
    return jnp.maximum(x, 0.0) + jnp.log(1.0 + jnp.exp(-jnp.abs(x)))


def _rms_heads(y, g):
    ms = jnp.mean(y * y, axis=-1, keepdims=True)
    return y * lax.rsqrt(ms + EPS) * g


def _rope(y, cos, sin_signed):
    lane = lax.broadcasted_iota(jnp.int32, y.shape, 1)
    fwd = pltpu.roll(y, 32, axis=1)
    bwd = pltpu.roll(y, 96, axis=1)
    partner = jnp.where((lane & 32) == 0, bwd, fwd)
    return y * cos + partner * sin_signed


def _inproj_kernel(x_ref, gn_ref, w_ref, cos_ref, sin_ref, gq_ref, gk_ref, gqm_ref, alog_ref, dtb_ref,
                   qkva_ref, za_ref, qrot_ref, kt_ref, v_ref, zb_ref, qc_ref, zc_ref, gates_ref,
                   scal_ref, scalt_ref, h_scr):
    x = x_ref[...]
    ms = jnp.mean(x * x, axis=-1, keepdims=True)
    h_scr[...] = (x * lax.rsqrt(ms + EPS) * gn_ref[...]).astype(BF16)

    def mm(off, n):
        return jnp.dot(h_scr[...], w_ref[:, off:off + n], preferred_element_type=F32)

    step = 512
    for c in range(0, N_QKV_A, step):
        qkva_ref[:, c:c + step] = mm(OFF_QKVA + c, step).astype(BF16)

    z = mm(OFF_ZA, W_A)
    za_ref[...] = (z * _sigmoid(z)).astype(BF16)

    cos = cos_ref[...]
    sin = sin_ref[...]
    for c in range(0, W_B, step):
        y = mm(OFF_QB + c, step)
        for j in range(step // HEAD_DIM):
            yh = _rope(_rms_heads(y[:, j * HEAD_DIM:(j + 1) * HEAD_DIM], gq_ref[...]), cos, sin)
            qrot_ref[:, c + j * HEAD_DIM:c + (j + 1) * HEAD_DIM] = (yh * SM_SCALE).astype(BF16)

    y = mm(OFF_KVB, N_KV_B)
    for g in range(ATT_KV_HEADS):
        kh = _rope(_rms_heads(y[:, g * HEAD_DIM:(g + 1) * HEAD_DIM], gk_ref[...]), cos, sin)
        kt_ref[0, g, 0] = kh.T.astype(BF16)
    v_ref[...] = y[:, ATT_KV_HEADS * HEAD_DIM:].astype(BF16)

    for c in range(0, W_B, step):
        z = mm(OFF_ZB + c, step)
        zb_ref[:, c:c + step] = (z * _sigmoid(z)).astype(BF16)

    y = mm(OFF_QC, W_C)
    for j in range(MEM_HEADS):
        yh = _rms_heads(y[:, j * HEAD_DIM:(j + 1) * HEAD_DIM], gqm_ref[...])
        qc_ref[:, j * HEAD_DIM:(j + 1) * HEAD_DIM] = (yh * SM_SCALE).astype(BF16)

    z = mm(OFF_ZC, W_C)
    zc_ref[...] = (z * _sigmoid(z)).astype(BF16)

    for c in range(0, N_GATE, step):
        gates_ref[:, c:c + step] = _sigmoid(mm(OFF_GATE + c, step)).astype(BF16)

    s = mm(OFF_SCAL, LANES)
    lane = lax.broadcasted_iota(jnp.int32, s.shape, 1)
    act = jnp.where(lane < 2 * GDN_HEADS, _sigmoid(s), -jnp.exp(alog_ref[...]) * _softplus(s + dtb_ref[...]))
    scal_ref[...] = act[:, :N_SCAL_A]
    scalt_ref[...] = act.T[:N_SCAL_A, :]


def _inproj(x, g_norm, w_pack, cos, sin, g_q, g_k, g_qm, alog, dtb, seq):
    ntok = x.shape[0]
    tm = TOK_BLOCK
    nblk = seq // tm
    nb = ntok // seq
    row = lambda i: (i, 0)
    const = lambda i: (0, 0)
    pos = lambda i: (i % nblk, 0)
    vec = pl.BlockSpec((1, LANES), const)
    out_shape = (
        jax.ShapeDtypeStruct((ntok, N_QKV_A), BF16),
        jax.ShapeDtypeStruct((ntok, W_A), BF16),
        jax.ShapeDtypeStruct((ntok, W_B), BF16),
        jax.ShapeDtypeStruct((nb, ATT_KV_HEADS, seq // ATT_TK, HEAD_DIM, ATT_TK), BF16),
        jax.ShapeDtypeStruct((ntok, ATT_KV_HEADS * HEAD_DIM), BF16),
        jax.ShapeDtypeStruct((ntok, W_B), BF16),
        jax.ShapeDtypeStruct((ntok, W_C), BF16),
        jax.ShapeDtypeStruct((ntok, W_C), BF16),
        jax.ShapeDtypeStruct((ntok, N_GATE), BF16),
        jax.ShapeDtypeStruct((ntok, N_SCAL_A), F32),
        jax.ShapeDtypeStruct((N_SCAL_A, ntok), F32),
    )
    out_specs = (
        pl.BlockSpec((tm, N_QKV_A), row),
        pl.BlockSpec((tm, W_A), row),
        pl.BlockSpec((tm, W_B), row),
        pl.BlockSpec((1, ATT_KV_HEADS, 1, HEAD_DIM, ATT_TK), lambda i: (i // nblk, 0, i % nblk, 0, 0)),
        pl.BlockSpec((tm, ATT_KV_HEADS * HEAD_DIM), row),
        pl.BlockSpec((tm, W_B), row),
        pl.BlockSpec((tm, W_C), row),
        pl.BlockSpec((tm, W_C), row),
        pl.BlockSpec((tm, N_GATE), row),
        pl.BlockSpec((tm, N_SCAL_A), row),
        pl.BlockSpec((N_SCAL_A, tm), lambda i: (0, i)),
    )
    in_specs = [
        pl.BlockSpec((tm, D_MODEL), row),
        pl.BlockSpec((1, D_MODEL), const),
        pl.BlockSpec((D_MODEL, N_PACK), const, pipeline_mode=pl.Buffered(1)),
        pl.BlockSpec((tm, LANES), pos),
        pl.BlockSpec((tm, LANES), pos),
        vec, vec, vec, vec, vec,
    ]
    return pl.pallas_call(
        _inproj_kernel,
        grid=(ntok // tm,),
        in_specs=in_specs,
        out_specs=out_specs,
        out_shape=out_shape,
        scratch_shapes=[pltpu.VMEM((tm, D_MODEL), BF16)],
        compiler_params=pltpu.CompilerParams(dimension_semantics=("parallel",), vmem_limit_bytes=VMEM_LIMIT),
        name="inproj",
    )(x, g_norm, w_pack, cos, sin, g_q, g_k, g_qm, alog, dtb)


HALO = 8


def _gdn_prep_kernel(prev_ref, cur_ref, next_ref, wc_ref, q_ref, k_ref, v_ref, xs_scr, *, nblk):
    tm = cur_ref.shape[0]
    sblk = pl.program_id(0) % nblk
    xs_scr[HALO:HALO + tm, :] = cur_ref[...].astype(F32)
    xs_scr[0:HALO, :] = jnp.where(sblk == 0, 0.0, prev_ref[...].astype(F32))
    xs_scr[HALO + tm:2 * HALO + tm, :] = jnp.where(sblk == nblk - 1, 0.0, next_ref[...].astype(F32))
    pad = (GDN_CONV - 1) // 2
    nqk = GDN_HEADS * GDN_DK
    for c in range(0, N_QKV_A, LANES):
        acc = None
        for j in range(GDN_CONV):
            t = xs_scr[HALO - pad + j:HALO - pad + j + tm, c:c + LANES] * wc_ref[j:j + 1, c:c + LANES]
            acc = t if acc is None else acc + t
        y = acc * _sigmoid(acc)
        if c < 2 * nqk:
            y = y * lax.rsqrt(jnp.sum(y * y, axis=-1, keepdims=True) + EPS)
        if c < nqk:
            q_ref[:, c:c + LANES] = (y * (GDN_DK ** -0.5)).astype(BF16)
        elif c < 2 * nqk:
            k_ref[:, c - nqk:c - nqk + LANES] = y.astype(BF16)
        else:
            v_ref[:, c - 2 * nqk:c - 2 * nqk + LANES] = y.astype(BF16)


def _gdn_prep(qkva, w_conv, seq):
    ntok = qkva.shape[0]
    tm = TOK_BLOCK
    nblk = seq // tm
    hb = tm // HALO
    nhalo = ntok // HALO
    out = jax.ShapeDtypeStruct((ntok, W_A), BF16)
    row = lambda i: (i, 0)
    return pl.pallas_call(
        functools.partial(_gdn_prep_kernel, nblk=nblk),
        grid=(ntok // tm,),
        in_specs=[
            pl.BlockSpec((HALO, N_QKV_A), lambda i: (jnp.maximum(i * hb - 1, 0), 0)),
            pl.BlockSpec((tm, N_QKV_A), row),
            pl.BlockSpec((HALO, N_QKV_A), lambda i: (jnp.minimum((i + 1) * hb, nhalo - 1), 0)),
            pl.BlockSpec((GDN_CONV, N_QKV_A), lambda i: (0, 0)),
        ],
        out_specs=(pl.BlockSpec((tm, W_A), row),) * 3,
        out_shape=(out, out, out),
        scratch_shapes=[pltpu.VMEM((tm + 2 * HALO, N_QKV_A), F32)],
        compiler_params=pltpu.CompilerParams(dimension_semantics=("parallel",), vmem_limit_bytes=VMEM_LIMIT),
        name="gdn_prep",
    )(qkva, qkva, qkva, w_conv)


def _dot(a, b):
    return jnp.dot(a, b, preferred_element_type=F32)


def _dot_exact(a, b):
    return jnp.dot(a, b, preferred_element_type=F32, precision=lax.Precision.HIGHEST)


def _gdn_direction(q_ref, k_ref, v_ref, sc_ref, sct_ref, o_ref, state_ref, reverse):
    n = GDN_BLOCK
    nchunk = n // GDN_CHUNK
    shift = GDN_CHUNK.bit_length() - 1
    row = lax.broadcasted_iota(jnp.int32, (n, n), 0)
    col = lax.broadcasted_iota(jnp.int32, (n, n), 1)
    same = (row >> shift) == (col >> shift)
    if reverse:
        incl = same & (row <= col)
        strict = same & (row < col)
        incl_t = same & (row >= col)
    else:
        incl = same & (row >= col)
        strict = same & (row > col)
        incl_t = same & (row <= col)
    one = jnp.float32(1.0)
    zero = jnp.float32(0.0)
    sc = sc_ref[...]
    sct = sct_ref[...]
    g_col_all = _dot_exact(jnp.where(incl, one, zero), sc)
    g_last_all = _dot_exact(jnp.where(same, one, zero), sc)
    g_row_all = _dot_exact(sct, jnp.where(incl_t, one, zero))
    eye = jnp.where(row == col, one, zero)
    rchunk = lax.broadcasted_iota(jnp.int32, (n, 2 * GDN_DV), 0) >> shift
    d = 1 if reverse else 0
    for h in range(GDN_HEADS):
        cb = d * GDN_HEADS + h
        cg = 2 * GDN_HEADS + d * GDN_HEADS + h
        beta = sc[:, cb:cb + 1]
        gc = g_col_all[:, cg:cg + 1]
        gl = g_last_all[:, cg:cg + 1]
        gr = g_row_all[cg:cg + 1, :]
        decay = jnp.where(incl, jnp.exp(jnp.where(incl, gc - gr, 0.0)), 0.0)
        hs = slice(h * GDN_DK, (h + 1) * GDN_DK)
        q = q_ref[:, hs].astype(F32)
        k_bf = k_ref[:, hs]
        k = k_bf.astype(F32)
        v = v_ref[:, hs].astype(F32)
        kb = k * beta
        e_g = jnp.exp(gc)
        lhs = jnp.concatenate([kb.astype(BF16), q.astype(BF16)], axis=0)
        kq = lax.dot_general(lhs, k_bf, (((1,), (1,)), ((), ())), preferred_element_type=F32)
        a_neg = jnp.where(strict, -(kq[:n] * decay), 0.0)
        qk = kq[n:] * decay
        t = eye + a_neg
        p = a_neg.astype(BF16)
        for it in range(shift - 1):
            p2 = _dot(p, p)
            p = p2.astype(BF16)
            t = t + _dot(t.astype(BF16), p)
        rhs = jnp.concatenate([(v * beta).astype(BF16), (kb * e_g).astype(BF16)], axis=1)
        uw = _dot(t.astype(BF16), rhs)
        qkuw = _dot(qk.astype(BF16), uw.astype(BF16))
        local = qkuw[:, :GDN_DV]
        q_eff = (q * e_g - qkuw[:, GDN_DV:]).astype(BF16)
        kdec_t = (k * jnp.exp(gl - gc)).T.astype(BF16)
        e_gl = jnp.exp(gl)
        state = state_ref[d * GDN_HEADS + h]
        order = range(nchunk - 1, -1, -1) if reverse else range(nchunk)
        for c in order:
            r0 = c * GDN_CHUNK
            kx = _dot(kdec_t, jnp.where(rchunk == c, uw, 0.0).astype(BF16))
            s_bf = state.astype(BF16)
            o_c = _dot(q_eff[r0:r0 + GDN_CHUNK], s_bf) + local[r0:r0 + GDN_CHUNK]
            o_ref[r0:r0 + GDN_CHUNK, h * GDN_DV:(h + 1) * GDN_DV] = o_c
            state = e_gl[r0:r0 + 1, :] * state - _dot(kx[:, GDN_DV:].astype(BF16), s_bf) + kx[:, :GDN_DV]
        state_ref[d * GDN_HEADS + h] = state


def _gdn_scan_kernel(qf_ref, kf_ref, vf_ref, scf_ref, sctf_ref, qb_ref, kb_ref, vb_ref, scb_ref, sctb_ref,
                     of_ref, ob_ref, state_ref):
    @pl.when(pl.program_id(1) == 0)
    def _():
        state_ref[...] = jnp.zeros_like(state_ref)

    _gdn_direction(qf_ref, kf_ref, vf_ref, scf_ref, sctf_ref, of_ref, state_ref, reverse=False)
    _gdn_direction(qb_ref, kb_ref, vb_ref, scb_ref, sctb_ref, ob_ref, state_ref, reverse=True)


def _gdn_scan(q, k, v, scal, scalt, seq):
    ntok = q.shape[0]
    n = GDN_BLOCK
    nt = seq // n
    nb = ntok // seq
    fwd = lambda b, j: (b * nt + j, 0)
    bwd = lambda b, j: (b * nt + nt - 1 - j, 0)
    fwd_t = lambda b, j: (0, b * nt + j)
    bwd_t = lambda b, j: (0, b * nt + nt - 1 - j)
    tok = lambda m: pl.BlockSpec((n, W_A), m)
    out = jax.ShapeDtypeStruct((ntok, W_A), F32)
    return pl.pallas_call(
        _gdn_scan_kernel,
        grid=(nb, nt),
        in_specs=[
            tok(fwd), tok(fwd), tok(fwd), pl.BlockSpec((n, N_SCAL_A), fwd), pl.BlockSpec((N_SCAL_A, n), fwd_t),
            tok(bwd), tok(bwd), tok(bwd), pl.BlockSpec((n, N_SCAL_A), bwd), pl.BlockSpec((N_SCAL_A, n), bwd_t),
        ],
        out_specs=(tok(fwd), tok(bwd)),
        out_shape=(out, out),
        scratch_shapes=[pltpu.VMEM((2 * GDN_HEADS, GDN_DK, GDN_DV), F32)],
        compiler_params=pltpu.CompilerParams(dimension_semantics=("parallel", "arbitrary"),
                                             vmem_limit_bytes=VMEM_LIMIT),
        name="gdn_scan",
    )(q, k, v, scal, scalt, q, k, v, scal, scalt)


def _attn_kernel(q_ref, kt_ref, v_ref, o_ref, m_scr, l_scr, acc_scr):
    grp = ATT_HEADS // ATT_KV_HEADS
    tq = q_ref.shape[1]
    nkv = kt_ref.shape[2]
    q = q_ref[0]
    qs = jnp.concatenate([q[:, h * HEAD_DIM:(h + 1) * HEAD_DIM] for h in range(grp)], axis=0)
    m_scr[...] = jnp.full_like(m_scr, -jnp.inf)
    l_scr[...] = jnp.zeros_like(l_scr)
    acc_scr[...] = jnp.zeros_like(acc_scr)

    def body(j, carry):
        s = _dot(qs, kt_ref[0, 0, j])
        m_prev = m_scr[...]
        m_new = jnp.maximum(m_prev, jnp.max(s, axis=-1, keepdims=True))
        alpha = jnp.exp2(m_prev - m_new)
        p = jnp.exp2(s - m_new)
        l_scr[...] = alpha * l_scr[...] + jnp.sum(p, axis=-1, keepdims=True)
        start = pl.multiple_of(j * ATT_TK, ATT_TK)
        acc_scr[...] = alpha * acc_scr[...] + _dot(p.astype(BF16), v_ref[0, pl.ds(start, ATT_TK), :])
        m_scr[...] = m_new
        return carry

    lax.fori_loop(0, nkv, body, 0)
    out = acc_scr[...] * (1.0 / l_scr[...])
    for h in range(grp):
        o_ref[0, :, h * HEAD_DIM:(h + 1) * HEAD_DIM] = out[h * tq:(h + 1) * tq].astype(o_ref.dtype)


def _attention(q_rot, kt, v, seq):
    nb = q_rot.shape[0]
    grp = ATT_HEADS // ATT_KV_HEADS
    tq = ATT_TQ
    gw = grp * HEAD_DIM
    return pl.pallas_call(
        _attn_kernel,
        grid=(nb, ATT_KV_HEADS, seq // tq),
        in_specs=[
            pl.BlockSpec((1, tq, gw), lambda b, g, i: (b, i, g)),
            pl.BlockSpec((1, 1, seq // ATT_TK, HEAD_DIM, ATT_TK), lambda b, g, i: (b, g, 0, 0, 0)),
            pl.BlockSpec((1, seq, HEAD_DIM), lambda b, g, i: (b, 0, g)),
        ],
        out_specs=pl.BlockSpec((1, tq, gw), lambda b, g, i: (b, i, g)),
        out_shape=jax.ShapeDtypeStruct((nb, seq, W_B), BF16),
        scratch_shapes=[
            pltpu.VMEM((grp * tq, 1), F32),
            pltpu.VMEM((grp * tq, 1), F32),
            pltpu.VMEM((grp * tq, HEAD_DIM), F32),
        ],
        compiler_params=pltpu.CompilerParams(dimension_semantics=("parallel", "parallel", "arbitrary"),
                                             vmem_limit_bytes=VMEM_LIMIT),
        name="attention",
    )(q_rot, kt, v)


def _memkv_kernel(mem_ref, gm_ref, w_ref, gk_ref, kt_ref, v_ref):
    m = mem_ref[0]
    ms = jnp.mean(m * m, axis=-1, keepdims=True)
    mn = (m * lax.rsqrt(ms + EPS) * gm_ref[...]).astype(BF16)
    kv = _dot(mn, w_ref[...])
    for h in range(MEM_HEADS):
        kh = _rms_heads(kv[:, h * HEAD_DIM:(h + 1) * HEAD_DIM], gk_ref[...])
        kt_ref[0, h] = kh.T.astype(BF16)
    v_ref[0] = kv[:, W_C:].astype(BF16)


def _memkv(mem, g_mem, w_mem_kv, g_k_mem):
    nb, n_mem, _ = mem.shape
    return pl.pallas_call(
        _memkv_kernel,
        grid=(nb,),
        in_specs=[
            pl.BlockSpec((1, n_mem, D_MODEL), lambda b: (b, 0, 0)),
            pl.BlockSpec((1, D_MODEL), lambda b: (0, 0)),
            pl.BlockSpec((D_MODEL, 2 * W_C), lambda b: (0, 0)),
            pl.BlockSpec((1, HEAD_DIM), lambda b: (0, 0)),
        ],
        out_specs=(
            pl.BlockSpec((1, MEM_HEADS, HEAD_DIM, n_mem), lambda b: (b, 0, 0, 0)),
            pl.BlockSpec((1, n_mem, W_C), lambda b: (b, 0, 0)),
        ),
        out_shape=(
            jax.ShapeDtypeStruct((nb, MEM_HEADS, HEAD_DIM, n_mem), BF16),
            jax.ShapeDtypeStruct((nb, n_mem, W_C), BF16),
        ),
        compiler_params=pltpu.CompilerParams(dimension_semantics=("parallel",), vmem_limit_bytes=VMEM_LIMIT),
        name="memkv",
    )(mem, g_mem, w_mem_kv, g_k_mem)


def _final_kernel(x_ref, of_ref, ob_ref, za_ref, att_ref, zb_ref, qc_ref, zc_ref, gates_ref, kmt_ref, vm_ref,
                  gout_ref, wda_ref, wdb_ref, wdc_ref, wout_ref, y_ref, oa_scr, oc_scr):
    o_gdn = of_ref[...] + ob_ref[...]
    for h in range(GDN_HEADS):
        hs = slice(h * GDN_DV, (h + 1) * GDN_DV)
        oa_scr[:, hs] = (_rms_heads(o_gdn[:, hs], gout_ref[...]) * za_ref[:, hs].astype(F32)).astype(BF16)
    ya = _dot(oa_scr[...], wda_ref[...])

    o_att = (att_ref[...].astype(F32) * zb_ref[...].astype(F32)).astype(BF16)
    yb = _dot(o_att, wdb_ref[...])

    for h in range(MEM_HEADS):
        hs = slice(h * HEAD_DIM, (h + 1) * HEAD_DIM)
        s = _dot(qc_ref[:, hs], kmt_ref[0, h])
        p = jnp.exp2(s - jnp.max(s, axis=-1, keepdims=True))
        l = jnp.sum(p, axis=-1, keepdims=True)
        o = _dot(p.astype(BF16), vm_ref[0, :, hs]) * (1.0 / l)
        oc_scr[:, hs] = (o * zc_ref[:, hs].astype(F32)).astype(BF16)
    yc = _dot(oc_scr[...], wdc_ref[...])

    mixed = (gates_ref[:, 0:D_MODEL].astype(F32) * ya
             + gates_ref[:, D_MODEL:2 * D_MODEL].astype(F32) * yb
             + gates_ref[:, 2 * D_MODEL:3 * D_MODEL].astype(F32) * yc)
    y_ref[...] = x_ref[...] + _dot(mixed.astype(BF16), wout_ref[...])


def _final(x, o_f, o_b, za, att, zb, qc, zc, gates, kmt, vm, g_out, wda, wdb, wdc, wout, seq):
    ntok = x.shape[0]
    tm = TOK_BLOCK
    nblk = seq // tm
    n_mem = vm.shape[1]
    row = lambda i: (i, 0)
    const = lambda i: (0, 0)
    tokspec = lambda w: pl.BlockSpec((tm, w), row)
    return pl.pallas_call(
        _final_kernel,
        grid=(ntok // tm,),
        in_specs=[
            tokspec(D_MODEL), tokspec(W_A), tokspec(W_A), tokspec(W_A), tokspec(W_B), tokspec(W_B),
            tokspec(W_C), tokspec(W_C), tokspec(N_GATE),
            pl.BlockSpec((1, MEM_HEADS, HEAD_DIM, n_mem), lambda i: (i // nblk, 0, 0, 0)),
            pl.BlockSpec((1, n_mem, W_C), lambda i: (i // nblk, 0, 0)),
            pl.BlockSpec((1, GDN_DV), const),
            pl.BlockSpec((W_A, D_MODEL), const),
            pl.BlockSpec((W_B, D_MODEL), const),
            pl.BlockSpec((W_C, D_MODEL), const),
            pl.BlockSpec((D_MODEL, D_MODEL), const),
        ],
        out_specs=tokspec(D_MODEL),
        out_shape=jax.ShapeDtypeStruct((ntok, D_MODEL), F32),
        scratch_shapes=[pltpu.VMEM((tm, W_A), BF16), pltpu.VMEM((tm, W_C), BF16)],
        compiler_params=pltpu.CompilerParams(dimension_semantics=("parallel",), vmem_limit_bytes=VMEM_LIMIT),
        name="final",
    )(x, o_f, o_b, za, att, zb, qc, zc, gates, kmt, vm, g_out, wda, wdb, wdc, wout)


def _rope_tables(seq):
    half = HEAD_DIM // 2
    freqs = ROPE_THETA ** (-jnp.arange(0, half, 2, dtype=F32) / half)
    t = jnp.arange(seq, dtype=jnp.int32)
    ang_r = (t // GRID_W).astype(F32)[:, None] * freqs[None, :]
    ang_c = (t % GRID_W).astype(F32)[:, None] * freqs[None, :]
    cos = jnp.concatenate([jnp.cos(ang_r)] * 2 + [jnp.cos(ang_c)] * 2, axis=-1)
    sin = jnp.concatenate([-jnp.sin(ang_r), jnp.sin(ang_r), -jnp.sin(ang_c), jnp.sin(ang_c)], axis=-1)
    return cos, sin


def _pad_lanes(v):
    return jnp.pad(v.astype(F32), (0, LANES - v.shape[0]))[None, :]


def _layer(x, mem, seq, g_norm, g_mem, w_in, w_conv, a_log_f, a_log_b, dt_bias_f, dt_bias_b, g_gdn_out,
           g_q_attn, g_k_attn, g_q_mem, g_k_mem, w_mem_kv, w_down_a, w_down_b, w_down_c, w_out):
    nb = x.shape[0] // seq
    o_s = N_QKV_A + W_A
    w_pack = jnp.concatenate(
        [w_in[:, :o_s], w_in[:, o_s + N_SCAL_A:], w_in[:, o_s:o_s + N_SCAL_A],
         jnp.zeros((D_MODEL, LANES - N_SCAL_A), w_in.dtype)], axis=1).astype(BF16)
    zeros8 = jnp.zeros((2 * GDN_HEADS,), F32)
    alog = _pad_lanes(jnp.concatenate([zeros8, a_log_f, a_log_b]))
    dtb = _pad_lanes(jnp.concatenate([zeros8, dt_bias_f, dt_bias_b]))
    cos, sin = _rope_tables(seq)

    (qkva, za, q_rot, kt, v_att, zb, qc, zc, gates, scal, scalt) = _inproj(
        x, g_norm[None, :], w_pack, cos, sin, g_q_attn[None, :], g_k_attn[None, :], g_q_mem[None, :], alog, dtb,
        seq)
    q_a, k_a, v_a = _gdn_prep(qkva, w_conv, seq)
    o_f, o_b = _gdn_scan(q_a, k_a, v_a, scal, scalt, seq)
    att = _attention(q_rot.reshape(nb, seq, W_B), kt, v_att.reshape(nb, seq, ATT_KV_HEADS * HEAD_DIM), seq)
    kmt, vm = _memkv(mem, g_mem[None, :], w_mem_kv.astype(BF16), g_k_mem[None, :])
    return _final(x, o_f, o_b, za, att.reshape(nb * seq, W_B), zb, qc, zc, gates, kmt, vm, g_gdn_out[None, :],
                  w_down_a.astype(BF16), w_down_b.astype(BF16), w_down_c.astype(BF16), w_out.astype(BF16), seq)


def kernel(x_prompt, x_sample, mem_prompt, mem_sample, g_norm, g_mem, w_in, w_conv, a_log_f, a_log_b, dt_bias_f,
           dt_bias_b, g_gdn_out, g_q_attn, g_k_attn, g_q_mem, g_k_mem, w_mem_kv, w_down_a, w_down_b, w_down_c,
           w_out):
    bp, seq, _ = x_prompt.shape
    bs, seq_s, _ = x_sample.shape
    assert seq == seq_s and seq % TOK_BLOCK == 0 and seq % ATT_TK == 0 and TOK_BLOCK == ATT_TK
    y = jnp.concatenate([x_prompt.reshape(bp * seq, D_MODEL), x_sample.reshape(bs * seq, D_MODEL)], axis=0)
    mem = jnp.concatenate([mem_prompt, mem_sample], axis=0)
    for l in range(g_norm.shape[0]):
        y = _layer(y, mem, seq, g_norm[l], g_mem[l], w_in[l], w_conv[l], a_log_f[l], a_log_b[l], dt_bias_f[l],
                   dt_bias_b[l], g_gdn_out[l], g_q_attn[l], g_k_attn[l], g_q_mem[l], g_k_mem[l], w_mem_kv[l],
                   w_down_a[l], w_down_b[l], w_down_c[l], w_out[l])
    return (y[:bp * seq].reshape(bp, seq, D_MODEL), y[bp * seq:].reshape(bs, seq, D_MODEL))
```

```python
import functools
import math

import jax
import jax.numpy as jnp
from jax import lax
from jax.experimental import pallas as pl
from jax.experimental.pallas import tpu as pltpu

F32 = jnp.float32
BF16 = jnp.bfloat16

D_MODEL = 1024
GRID_W = 64
HEAD_DIM = 128
EPS = 1e-6
GDN_HEADS = 4
GDN_DK = 128
GDN_DV = 128
GDN_CONV = 5
GDN_CHUNK = 64
ATT_HEADS = 8
ATT_KV_HEADS = 2
ROPE_THETA = 10000.0
MEM_HEADS = 4
N_BRANCH = 3

W_A = GDN_HEADS * GDN_DV
W_B = ATT_HEADS * HEAD_DIM
W_C = MEM_HEADS * HEAD_DIM
N_QKV_A = 2 * GDN_HEADS * GDN_DK + GDN_HEADS * GDN_DV
N_SCAL_A = 4 * GDN_HEADS
N_KV_B = 2 * ATT_KV_HEADS * HEAD_DIM
N_GATE = N_BRANCH * D_MODEL

LANES = 128
LOG2E = math.log2(math.e)
SM_SCALE = HEAD_DIM ** -0.5 * LOG2E

OFF_QKVA = 0
OFF_ZA = OFF_QKVA + N_QKV_A
OFF_QB = OFF_ZA + W_A
OFF_KVB = OFF_QB + W_B
OFF_ZB = OFF_KVB + N_KV_B
OFF_QC = OFF_ZB + W_B
OFF_ZC = OFF_QC + W_C
OFF_GATE = OFF_ZC + W_C
OFF_SCAL = OFF_GATE + N_GATE
N_PACK = OFF_SCAL + LANES

TOK_BLOCK = 512
GDN_BLOCK = 256
ATT_TQ = 256
ATT_TK = 512
N_SPLIT = 3
VMEM_LIMIT = 56 * 1024 * 1024


def _sigmoid(x):
    return 1.0 / (1.0 + jnp.exp(-x))


def _softplus(x):
    return jnp.maximum(x, 0.0) + jnp.log(1.0 + jnp.exp(-jnp.abs(x)))


def _rms_heads(y, g):
    ms = jnp.mean(y * y, axis=-1, keepdims=True)
    return y * lax.rsqrt(ms + EPS) * g


def _rope(y, cos, sin_signed):
    lane = lax.broadcasted_iota(jnp.int32, y.shape, 1)
    fwd = pltpu.roll(y, 32, axis=1)
    bwd = pltpu.roll(y, 96, axis=1)
    partner = jnp.where((lane & 32) == 0, bwd, fwd)
    return y * cos + partner * sin_signed


def _dot(a, b):
    return jnp.dot(a, b, preferred_element_type=F32)


def _inproj_kernel(xp_ref, xs_ref, gn_ref, w_ref, cos_ref, sin_ref, gq_ref, gk_ref, gqm_ref, alog_ref, dtb_ref,
                   qkva_ref, za_ref, qt_ref, k_ref, vt_ref, zb_ref, qc_ref, zc_ref, gates_ref,
                   scal_ref, sc3_ref, sct3_ref, h_scr, *, n_prompt_blocks):
    def norm_into_scratch(x_ref):
        x = x_ref[...]
        ms = jnp.mean(x * x, axis=-1, keepdims=True)
        h_scr[...] = (x * lax.rsqrt(ms + EPS) * gn_ref[...]).astype(BF16)

    @pl.when(pl.program_id(0) < n_prompt_blocks)
    def _():
        norm_into_scratch(xp_ref)

    @pl.when(pl.program_id(0) >= n_prompt_blocks)
    def _():
        norm_into_scratch(xs_ref)

    def mm(off, n):
        return _dot(h_scr[...], w_ref[:, off:off + n])

    step = 512
    for c in range(0, N_QKV_A, step):
        qkva_ref[:, c:c + step] = mm(OFF_QKVA + c, step).astype(BF16)

    z = mm(OFF_ZA, W_A)
    za_ref[...] = (z * _sigmoid(z)).astype(BF16)

    cos = cos_ref[...]
    sin = sin_ref[...]
    for c in range(0, W_B, step):
        y = mm(OFF_QB + c, step)
        for j in range(step // HEAD_DIM):
            yh = _rope(_rms_heads(y[:, j * HEAD_DIM:(j + 1) * HEAD_DIM], gq_ref[...]), cos, sin)
            qt_ref[0, c // HEAD_DIM + j] = (yh * SM_SCALE).T.astype(BF16)

    y = mm(OFF_KVB, N_KV_B)
    for g in range(ATT_KV_HEADS):
        kh = _rope(_rms_heads(y[:, g * HEAD_DIM:(g + 1) * HEAD_DIM], gk_ref[...]), cos, sin)
        k_ref[:, g * HEAD_DIM:(g + 1) * HEAD_DIM] = kh.astype(BF16)
        vh = y[:, (ATT_KV_HEADS + g) * HEAD_DIM:(ATT_KV_HEADS + g + 1) * HEAD_DIM]
        vt_ref[0, g, 0] = vh.T.astype(BF16)

    for c in range(0, W_B, step):
        z = mm(OFF_ZB + c, step)
        zb_ref[:, c:c + step] = (z * _sigmoid(z)).astype(BF16)

    y = mm(OFF_QC, W_C)
    for j in range(MEM_HEADS):
        yh = _rms_heads(y[:, j * HEAD_DIM:(j + 1) * HEAD_DIM], gqm_ref[...])
        qc_ref[:, j * HEAD_DIM:(j + 1) * HEAD_DIM] = (yh * SM_SCALE).astype(BF16)

    z = mm(OFF_ZC, W_C)
    zc_ref[...] = (z * _sigmoid(z)).astype(BF16)

    for c in range(0, N_GATE, step):
        gates_ref[:, c:c + step] = _sigmoid(mm(OFF_GATE + c, step)).astype(BF16)

    s = mm(OFF_SCAL, LANES)
    lane = lax.broadcasted_iota(jnp.int32, s.shape, 1)
    act = jnp.where(lane < 2 * GDN_HEADS, _sigmoid(s), -jnp.exp(alog_ref[...]) * _softplus(s + dtb_ref[...]))
    act = jnp.where(lane < N_SCAL_A, act, 0.0)
    scal_ref[...] = act[:, :N_SCAL_A]
    hi = act.astype(BF16).astype(F32)
    rem = act - hi
    mid = rem.astype(BF16).astype(F32)
    low = rem - mid
    packed = hi + pltpu.roll(mid, N_SCAL_A, axis=1) + pltpu.roll(low, 2 * N_SCAL_A, axis=1)
    sc3_ref[...] = packed.astype(BF16)
    sct3_ref[...] = packed.T[:N_SPLIT * N_SCAL_A, :].astype(BF16)


def _inproj(xp, xs, g_norm, w_pack, cos, sin, g_q, g_k, g_qm, alog, dtb, seq):
    tm = TOK_BLOCK
    npb = xp.shape[0] // tm
    ntok = xp.shape[0] + xs.shape[0]
    nblk = seq // tm
    nb = ntok // seq
    row = lambda i: (i, 0)
    const = lambda i: (0, 0)
    pos = lambda i: (i % nblk, 0)
    vec = pl.BlockSpec((1, LANES), const)
    out_shape = (
        jax.ShapeDtypeStruct((ntok, N_QKV_A), BF16),
        jax.ShapeDtypeStruct((ntok, W_A), BF16),
        jax.ShapeDtypeStruct((nb, ATT_HEADS, HEAD_DIM, seq), BF16),
        jax.ShapeDtypeStruct((ntok, ATT_KV_HEADS * HEAD_DIM), BF16),
        jax.ShapeDtypeStruct((nb, ATT_KV_HEADS, seq // ATT_TK, HEAD_DIM, ATT_TK), BF16),
        jax.ShapeDtypeStruct((ntok, W_B), BF16),
        jax.ShapeDtypeStruct((ntok, W_C), BF16),
        jax.ShapeDtypeStruct((ntok, W_C), BF16),
        jax.ShapeDtypeStruct((ntok, N_GATE), BF16),
        jax.ShapeDtypeStruct((ntok, N_SCAL_A), F32),
        jax.ShapeDtypeStruct((ntok, LANES), BF16),
        jax.ShapeDtypeStruct((N_SPLIT * N_SCAL_A, ntok), BF16),
    )
    out_specs = (
        pl.BlockSpec((tm, N_QKV_A), row),
        pl.BlockSpec((tm, W_A), row),
        pl.BlockSpec((1, ATT_HEADS, HEAD_DIM, tm), lambda i: (i // nblk, 0, 0, i % nblk)),
        pl.BlockSpec((tm, ATT_KV_HEADS * HEAD_DIM), row),
        pl.BlockSpec((1, ATT_KV_HEADS, 1, HEAD_DIM, ATT_TK), lambda i: (i // nblk, 0, i % nblk, 0, 0)),
        pl.BlockSpec((tm, W_B), row),
        pl.BlockSpec((tm, W_C), row),
        pl.BlockSpec((tm, W_C), row),
        pl.BlockSpec((tm, N_GATE), row),
        pl.BlockSpec((tm, N_SCAL_A), row),
        pl.BlockSpec((tm, LANES), row),
        pl.BlockSpec((N_SPLIT * N_SCAL_A, tm), lambda i: (0, i)),
    )
    in_specs = [
        pl.BlockSpec((tm, D_MODEL), lambda i: (jnp.minimum(i, npb - 1), 0)),
        pl.BlockSpec((tm, D_MODEL), lambda i: (jnp.maximum(i - npb, 0), 0)),
        pl.BlockSpec((1, D_MODEL), const),
        pl.BlockSpec((D_MODEL, N_PACK), const, pipeline_mode=pl.Buffered(1)),
        pl.BlockSpec((tm, LANES), pos),
        pl.BlockSpec((tm, LANES), pos),
        vec, vec, vec, vec, vec,
    ]
    return pl.pallas_call(
        functools.partial(_inproj_kernel, n_prompt_blocks=npb),
        grid=(ntok // tm,),
        in_specs=in_specs,
        out_specs=out_specs,
        out_shape=out_shape,
        scratch_shapes=[pltpu.VMEM((tm, D_MODEL), BF16)],
        compiler_params=pltpu.CompilerParams(dimension_semantics=("parallel",), vmem_limit_bytes=VMEM_LIMIT),
        name="inproj",
    )(xp, xs, g_norm, w_pack, cos, sin, g_q, g_k, g_qm, alog, dtb)


HALO = 8


def _gdn_prep_kernel(prev_ref, cur_ref, next_ref, wc_ref, q_ref, k_ref, v_ref, xs_scr, *, nblk):
    tm = cur_ref.shape[0]
    sblk = pl.program_id(0) % nblk
    xs_scr[HALO:HALO + tm, :] = cur_ref[...].astype(F32)
    xs_scr[0:HALO, :] = jnp.where(sblk == 0, 0.0, prev_ref[...].astype(F32))
    xs_scr[HALO + tm:2 * HALO + tm, :] = jnp.where(sblk == nblk - 1, 0.0, next_ref[...].astype(F32))
    pad = (GDN_CONV - 1) // 2
    nqk = GDN_HEADS * GDN_DK
    for c in range(0, N_QKV_A, LANES):
        acc = None
        for j in range(GDN_CONV):
            t = xs_scr[HALO - pad + j:HALO - pad + j + tm, c:c + LANES] * wc_ref[j:j + 1, c:c + LANES]
            acc = t if acc is None else acc + t
        y = acc * _sigmoid(acc)
        if c < 2 * nqk:
            y = y * lax.rsqrt(jnp.sum(y * y, axis=-1, keepdims=True) + EPS)
        if c < nqk:
            q_ref[:, c:c + LANES] = (y * (GDN_DK ** -0.5)).astype(BF16)
        elif c < 2 * nqk:
            k_ref[:, c - nqk:c - nqk + LANES] = y.astype(BF16)
        else:
            v_ref[:, c - 2 * nqk:c - 2 * nqk + LANES] = y.astype(BF16)


def _gdn_prep(qkva, w_conv, seq):
    ntok = qkva.shape[0]
    tm = TOK_BLOCK
    nblk = seq // tm
    hb = tm // HALO
    nhalo = ntok // HALO
    out = jax.ShapeDtypeStruct((ntok, W_A), BF16)
    row = lambda i: (i, 0)
    return pl.pallas_call(
        functools.partial(_gdn_prep_kernel, nblk=nblk),
        grid=(ntok // tm,),
        in_specs=[
            pl.BlockSpec((HALO, N_QKV_A), lambda i: (jnp.maximum(i * hb - 1, 0), 0)),
            pl.BlockSpec((tm, N_QKV_A), row),
            pl.BlockSpec((HALO, N_QKV_A), lambda i: (jnp.minimum((i + 1) * hb, nhalo - 1), 0)),
            pl.BlockSpec((GDN_CONV, N_QKV_A), lambda i: (0, 0)),
        ],
        out_specs=(pl.BlockSpec((tm, W_A), row),) * 3,
        out_shape=(out, out, out),
        scratch_shapes=[pltpu.VMEM((tm + 2 * HALO, N_QKV_A), F32)],
        compiler_params=pltpu.CompilerParams(dimension_semantics=("parallel",), vmem_limit_bytes=VMEM_LIMIT),
        name="gdn_prep",
    )(qkva, qkva, qkva, w_conv)


def _gdn_scan_kernel(qf_ref, kf_ref, vf_ref, scf_ref, sc3f_ref, sct3f_ref,
                     qb_ref, kb_ref, vb_ref, scb_ref, sc3b_ref, sct3b_ref, of_ref, ob_ref, state_ref):
    @pl.when(pl.program_id(1) == 0)
    def _():
        state_ref[...] = jnp.zeros_like(state_ref)

    n = GDN_BLOCK
    nchunk = n // GDN_CHUNK
    shift = GDN_CHUNK.bit_length() - 1
    row = lax.broadcasted_iota(jnp.int32, (n, n), 0)
    col = lax.broadcasted_iota(jnp.int32, (n, n), 1)
    same = (row >> shift) == (col >> shift)
    one = jnp.float32(1.0)
    zero = jnp.float32(0.0)
    eye = jnp.where(row == col, one, zero)
    rchunk = lax.broadcasted_iota(jnp.int32, (n, 2 * GDN_DV), 0) >> shift
    nsc = N_SCAL_A

    probs = []
    for d, (q_ref, k_ref, v_ref, sc_ref, sc3_ref, sct3_ref, o_ref) in enumerate((
            (qf_ref, kf_ref, vf_ref, scf_ref, sc3f_ref, sct3f_ref, of_ref),
            (qb_ref, kb_ref, vb_ref, scb_ref, sc3b_ref, sct3b_ref, ob_ref))):
        if d == 1:
            incl = same & (row <= col)
            strict = same & (row < col)
            incl_t = same & (row >= col)
        else:
            incl = same & (row >= col)
            strict = same & (row > col)
            incl_t = same & (row <= col)
        m01 = jnp.concatenate([jnp.where(incl, one, zero), jnp.where(same, one, zero)], axis=0).astype(BF16)
        gg = _dot(m01, sc3_ref[...])
        gg = gg + pltpu.roll(gg, LANES - nsc, axis=1) + pltpu.roll(gg, LANES - 2 * nsc, axis=1)
        gr3 = _dot(sct3_ref[...], jnp.where(incl_t, one, zero).astype(BF16))
        g_row_all = gr3[0:nsc] + gr3[nsc:2 * nsc] + gr3[2 * nsc:3 * nsc]
        sc = sc_ref[...]
        for h in range(GDN_HEADS):
            cb = d * GDN_HEADS + h
            cg = 2 * GDN_HEADS + cb
            hs = slice(h * GDN_DK, (h + 1) * GDN_DK)
            probs.append(dict(
                idx=cb, h=h, reverse=d == 1, incl=incl, strict=strict, o_ref=o_ref,
                beta=sc[:, cb:cb + 1], gc=gg[:n, cg:cg + 1], gl=gg[n:, cg:cg + 1], gr=g_row_all[cg:cg + 1, :],
                q_ref=q_ref, k_ref=k_ref, v_ref=v_ref, hs=hs))

    for p in probs:
        incl = p["incl"]
        decay = jnp.where(incl, jnp.exp(jnp.where(incl, p["gc"] - p["gr"], 0.0)), 0.0)
        k_bf = p["k_ref"][:, p["hs"]]
        q_bf = p["q_ref"][:, p["hs"]]
        kb = k_bf.astype(F32) * p["beta"]
        lhs = jnp.concatenate([kb.astype(BF16), q_bf], axis=0)
        kq = lax.dot_general(lhs, k_bf, (((1,), (1,)), ((), ())), preferred_element_type=F32)
        a_neg = jnp.where(p["strict"], -(kq[:n] * decay), 0.0)
        p["qk"] = (kq[n:] * decay).astype(BF16)
        p["t"] = eye + a_neg
        p["pw"] = a_neg.astype(BF16)
        p["kb"] = kb

    for _ in range(shift - 1):
        for p in probs:
            p["pw"] = _dot(p["pw"], p["pw"]).astype(BF16)
        for p in probs:
            p["t"] = p["t"] + _dot(p["t"].astype(BF16), p["pw"])

    for p in probs:
        e_g = jnp.exp(p["gc"])
        v = p["v_ref"][:, p["hs"]].astype(F32)
        rhs = jnp.concatenate([(v * p["beta"]).astype(BF16), (p["kb"] * e_g).astype(BF16)], axis=1)
        p["uw"] = _dot(p["t"].astype(BF16), rhs)
        p["e_g"] = e_g

    for p in probs:
        qkuw = _dot(p["qk"], p["uw"].astype(BF16))
        p["local"] = qkuw[:, :GDN_DV]
        q = p["q_ref"][:, p["hs"]].astype(F32)
        p["q_eff"] = (q * p["e_g"] - qkuw[:, GDN_DV:]).astype(BF16)
        k = p["k_ref"][:, p["hs"]].astype(F32)
        p["kdec_t"] = (k * jnp.exp(p["gl"] - p["gc"])).T.astype(BF16)
        p["e_gl"] = jnp.exp(p["gl"])

    for p in probs:
        p["kx"] = [_dot(p["kdec_t"], jnp.where(rchunk == c, p["uw"], 0.0).astype(BF16)) for c in range(nchunk)]
        p["state"] = state_ref[p["idx"]]

    for step in range(nchunk):
        for p in probs:
            c = nchunk - 1 - step if p["reverse"] else step
            r0 = c * GDN_CHUNK
            s_bf = p["state"].astype(BF16)
            o_c = _dot(p["q_eff"][r0:r0 + GDN_CHUNK], s_bf) + p["local"][r0:r0 + GDN_CHUNK]
            p["o_ref"][r0:r0 + GDN_CHUNK, p["hs"]] = o_c
            kx = p["kx"][c]
            p["state"] = (p["e_gl"][r0:r0 + 1, :] * p["state"] - _dot(kx[:, GDN_DV:].astype(BF16), s_bf)
                          + kx[:, :GDN_DV])

    for p in probs:
        state_ref[p["idx"]] = p["state"]


def _gdn_scan(q, k, v, scal, sc3, sct3, seq):
    ntok = q.shape[0]
    n = GDN_BLOCK
    nt = seq // n
    nb = ntok // seq
    fwd = lambda b, j: (b * nt + j, 0)
    bwd = lambda b, j: (b * nt + nt - 1 - j, 0)
    fwd_t = lambda b, j: (0, b * nt + j)
    bwd_t = lambda b, j: (0, b * nt + nt - 1 - j)
    tok = lambda m: pl.BlockSpec((n, W_A), m)
    side = lambda m, mt: [tok(m), tok(m), tok(m), pl.BlockSpec((n, N_SCAL_A), m), pl.BlockSpec((n, LANES), m),
                          pl.BlockSpec((N_SPLIT * N_SCAL_A, n), mt)]
    out = jax.ShapeDtypeStruct((ntok, W_A), F32)
    return pl.pallas_call(
        _gdn_scan_kernel,
        grid=(nb, nt),
        in_specs=side(fwd, fwd_t) + side(bwd, bwd_t),
        out_specs=(tok(fwd), tok(bwd)),
        out_shape=(out, out),
        scratch_shapes=[pltpu.VMEM((2 * GDN_HEADS, GDN_DK, GDN_DV), F32)],
        compiler_params=pltpu.CompilerParams(dimension_semantics=("parallel", "arbitrary"),
                                             vmem_limit_bytes=VMEM_LIMIT),
        name="gdn_scan",
    )(q, k, v, scal, sc3, sct3, q, k, v, scal, sc3, sct3)


def _attn_kernel(qt_ref, k_ref, vt_ref, o_ref, m_scr, l_scr, acc_scr, s_scr):
    grp = qt_ref.shape[1]
    nkv = vt_ref.shape[2]
    m_scr[...] = jnp.full_like(m_scr, -jnp.inf)
    l_scr[...] = jnp.zeros_like(l_scr)
    acc_scr[...] = jnp.zeros_like(acc_scr)

    def scores(j, slot, h):
        start = pl.multiple_of(j * ATT_TK, ATT_TK)
        s_scr[slot, h] = _dot(k_ref[0, pl.ds(start, ATT_TK), :], qt_ref[0, h])

    def update(j, slot, h):
        m_prev = m_scr[h]
        m_new = jnp.maximum(m_prev, jnp.max(s_scr[slot, h], axis=0, keepdims=True))
        alpha = jnp.exp2(m_prev - m_new)
        p = jnp.exp2(s_scr[slot, h] - m_new)
        l_scr[h] = alpha * l_scr[h] + jnp.sum(p, axis=0, keepdims=True)
        acc_scr[h] = alpha * acc_scr[h] + _dot(vt_ref[0, 0, j], p.astype(BF16))
        m_scr[h] = m_new

    def stage(j, slot):
        j_next = jnp.minimum(j + 1, nkv - 1)
        for h in range(0, grp, 2):
            scores(j_next, 1 - slot, h)
            scores(j_next, 1 - slot, h + 1)
            update(j, slot, h)
            update(j, slot, h + 1)

    for h in range(grp):
        scores(0, 0, h)

    def body(i, carry):
        stage(2 * i, 0)
        stage(2 * i + 1, 1)
        return carry

    lax.fori_loop(0, nkv // 2, body, 0)
    for h in range(grp):
        out = acc_scr[h] * (1.0 / l_scr[h])
        o_ref[0, :, h * HEAD_DIM:(h + 1) * HEAD_DIM] = out.T.astype(o_ref.dtype)


def _attention(qt, k, vt, seq):
    nb = qt.shape[0]
    grp = ATT_HEADS // ATT_KV_HEADS
    tq = ATT_TQ
    return pl.pallas_call(
        _attn_kernel,
        grid=(nb, ATT_KV_HEADS, seq // tq),
        in_specs=[
            pl.BlockSpec((1, grp, HEAD_DIM, tq), lambda b, g, i: (b, g, 0, i)),
            pl.BlockSpec((1, seq, HEAD_DIM), lambda b, g, i: (b, 0, g)),
            pl.BlockSpec((1, 1, seq // ATT_TK, HEAD_DIM, ATT_TK), lambda b, g, i: (b, g, 0, 0, 0)),
        ],
        out_specs=pl.BlockSpec((1, tq, grp * HEAD_DIM), lambda b, g, i: (b, i, g)),
        out_shape=jax.ShapeDtypeStruct((nb, seq, W_B), BF16),
        scratch_shapes=[
            pltpu.VMEM((grp, 1, tq), F32),
            pltpu.VMEM((grp, 1, tq), F32),
            pltpu.VMEM((grp, HEAD_DIM, tq), F32),
            pltpu.VMEM((2, grp, ATT_TK, tq), F32),
        ],
        compiler_params=pltpu.CompilerParams(dimension_semantics=("parallel", "parallel", "arbitrary"),
                                             vmem_limit_bytes=VMEM_LIMIT),
        name="attention",
    )(qt, k, vt)


def _memkv_kernel(mem_ref, gm_ref, w_ref, gk_ref, kt_ref, v_ref):
    m = mem_ref[0]
    ms = jnp.mean(m * m, axis=-1, keepdims=True)
    mn = (m * lax.rsqrt(ms + EPS) * gm_ref[...]).astype(BF16)
    kv = _dot(mn, w_ref[...])
    for h in range(MEM_HEADS):
        kh = _rms_heads(kv[:, h * HEAD_DIM:(h + 1) * HEAD_DIM], gk_ref[...])
        kt_ref[0, h] = kh.T.astype(BF16)
    v_ref[0] = kv[:, W_C:].astype(BF16)


def _memkv(mem, g_mem, w_mem_kv, g_k_mem):
    nb, n_mem, _ = mem.shape
    return pl.pallas_call(
        _memkv_kernel,
        grid=(nb,),
        in_specs=[
            pl.BlockSpec((1, n_mem, D_MODEL), lambda b: (b, 0, 0)),
            pl.BlockSpec((1, D_MODEL), lambda b: (0, 0)),
            pl.BlockSpec((D_MODEL, 2 * W_C), lambda b: (0, 0)),
            pl.BlockSpec((1, HEAD_DIM), lambda b: (0, 0)),
        ],
        out_specs=(
            pl.BlockSpec((1, MEM_HEADS, HEAD_DIM, n_mem), lambda b: (b, 0, 0, 0)),
            pl.BlockSpec((1, n_mem, W_C), lambda b: (b, 0, 0)),
        ),
        out_shape=(
            jax.ShapeDtypeStruct((nb, MEM_HEADS, HEAD_DIM, n_mem), BF16),
            jax.ShapeDtypeStruct((nb, n_mem, W_C), BF16),
        ),
        compiler_params=pltpu.CompilerParams(dimension_semantics=("parallel",), vmem_limit_bytes=VMEM_LIMIT),
        name="memkv",
    )(mem, g_mem, w_mem_kv, g_k_mem)


def _final_kernel(xp_ref, xs_ref, of_ref, ob_ref, za_ref, att_ref, zb_ref, qc_ref, zc_ref, gates_ref, kmt_ref,
                  vm_ref, gout_ref, wda_ref, wdb_ref, wdc_ref, wout_ref, yp_ref, ys_ref, oa_scr, oc_scr,
                  *, n_prompt_blocks):
    o_gdn = of_ref[...] + ob_ref[...]
    for h in range(GDN_HEADS):
        hs = slice(h * GDN_DV, (h + 1) * GDN_DV)
        oa_scr[:, hs] = (_rms_heads(o_gdn[:, hs], gout_ref[...]) * za_ref[:, hs].astype(F32)).astype(BF16)
    ya = _dot(oa_scr[...], wda_ref[...])

    o_att = (att_ref[...].astype(F32) * zb_ref[...].astype(F32)).astype(BF16)
    yb = _dot(o_att, wdb_ref[...])

    for h in range(MEM_HEADS):
        hs = slice(h * HEAD_DIM, (h + 1) * HEAD_DIM)
        s = _dot(qc_ref[:, hs], kmt_ref[0, h])
        p = jnp.exp2(s - jnp.max(s, axis=-1, keepdims=True))
        l = jnp.sum(p, axis=-1, keepdims=True)
        o = _dot(p.astype(BF16), vm_ref[0, :, hs]) * (1.0 / l)
        oc_scr[:, hs] = (o * zc_ref[:, hs].astype(F32)).astype(BF16)
    yc = _dot(oc_scr[...], wdc_ref[...])

    mixed = (gates_ref[:, 0:D_MODEL].astype(F32) * ya
             + gates_ref[:, D_MODEL:2 * D_MODEL].astype(F32) * yb
             + gates_ref[:, 2 * D_MODEL:3 * D_MODEL].astype(F32) * yc)
    delta = _dot(mixed.astype(BF16), wout_ref[...])

    @pl.when(pl.program_id(0) < n_prompt_blocks)
    def _():
        yp_ref[...] = xp_ref[...] + delta

    @pl.when(pl.program_id(0) >= n_prompt_blocks)
    def _():
        ys_ref[...] = xs_ref[...] + delta


def _final(xp, xs, o_f, o_b, za, att, zb, qc, zc, gates, kmt, vm, g_out, wda, wdb, wdc, wout, seq):
    tm = TOK_BLOCK
    npb = xp.shape[0] // tm
    ntok = xp.shape[0] + xs.shape[0]
    nblk = seq // tm
    n_mem = vm.shape[1]
    row = lambda i: (i, 0)
    const = lambda i: (0, 0)
    prompt = lambda i: (jnp.minimum(i, npb - 1), 0)
    sample = lambda i: (jnp.maximum(i - npb, 0), 0)
    tokspec = lambda w: pl.BlockSpec((tm, w), row)
    return pl.pallas_call(
        functools.partial(_final_kernel, n_prompt_blocks=npb),
        grid=(ntok // tm,),
        in_specs=[
            pl.BlockSpec((tm, D_MODEL), prompt), pl.BlockSpec((tm, D_MODEL), sample),
            tokspec(W_A), tokspec(W_A), tokspec(W_A), tokspec(W_B), tokspec(W_B),
            tokspec(W_C), tokspec(W_C), tokspec(N_GATE),
            pl.BlockSpec((1, MEM_HEADS, HEAD_DIM, n_mem), lambda i: (i // nblk, 0, 0, 0)),
            pl.BlockSpec((1, n_mem, W_C), lambda i: (i // nblk, 0, 0)),
            pl.BlockSpec((1, GDN_DV), const),
            pl.BlockSpec((W_A, D_MODEL), const),
            pl.BlockSpec((W_B, D_MODEL), const),
            pl.BlockSpec((W_C, D_MODEL), const),
            pl.BlockSpec((D_MODEL, D_MODEL), const),
        ],
        out_specs=(pl.BlockSpec((tm, D_MODEL), prompt), pl.BlockSpec((tm, D_MODEL), sample)),
        out_shape=(jax.ShapeDtypeStruct(xp.shape, F32), jax.ShapeDtypeStruct(xs.shape, F32)),
        scratch_shapes=[pltpu.VMEM((tm, W_A), BF16), pltpu.VMEM((tm, W_C), BF16)],
        compiler_params=pltpu.CompilerParams(dimension_semantics=("arbitrary",), vmem_limit_bytes=VMEM_LIMIT),
        name="final",
    )(xp, xs, o_f, o_b, za, att, zb, qc, zc, gates, kmt, vm, g_out, wda, wdb, wdc, wout)


def _rope_tables(seq):
    half = HEAD_DIM // 2
    freqs = ROPE_THETA ** (-jnp.arange(0, half, 2, dtype=F32) / half)
    t = jnp.arange(seq, dtype=jnp.int32)
    ang_r = (t // GRID_W).astype(F32)[:, None] * freqs[None, :]
    ang_c = (t % GRID_W).astype(F32)[:, None] * freqs[None, :]
    cos = jnp.concatenate([jnp.cos(ang_r)] * 2 + [jnp.cos(ang_c)] * 2, axis=-1)
    sin = jnp.concatenate([-jnp.sin(ang_r), jnp.sin(ang_r), -jnp.sin(ang_c), jnp.sin(ang_c)], axis=-1)
    return cos, sin


def _pad_lanes(v):
    return jnp.pad(v.astype(F32), (0, LANES - v.shape[0]))[None, :]


def _layer(xp, xs, mem, seq, g_norm, g_mem, w_in, w_conv, a_log_f, a_log_b, dt_bias_f, dt_bias_b, g_gdn_out,
           g_q_attn, g_k_attn, g_q_mem, g_k_mem, w_mem_kv, w_down_a, w_down_b, w_down_c, w_out):
    o_s = N_QKV_A + W_A
    w_pack = jnp.concatenate(
        [w_in[:, :o_s], w_in[:, o_s + N_SCAL_A:], w_in[:, o_s:o_s + N_SCAL_A],
         jnp.zeros((D_MODEL, LANES - N_SCAL_A), w_in.dtype)], axis=1).astype(BF16)
    zeros8 = jnp.zeros((2 * GDN_HEADS,), F32)
    alog = _pad_lanes(jnp.concatenate([zeros8, a_log_f, a_log_b]))
    dtb = _pad_lanes(jnp.concatenate([zeros8, dt_bias_f, dt_bias_b]))
    cos, sin = _rope_tables(seq)

    (qkva, za, qt, k_att, vt, zb, qc, zc, gates, scal, sc3, sct3) = _inproj(
        xp, xs, g_norm[None, :], w_pack, cos, sin, g_q_attn[None, :], g_k_attn[None, :], g_q_mem[None, :], alog,
        dtb, seq)
    nb = qt.shape[0]
    q_a, k_a, v_a = _gdn_prep(qkva, w_conv, seq)
    o_f, o_b = _gdn_scan(q_a, k_a, v_a, scal, sc3, sct3, seq)
    att = _attention(qt, k_att.reshape(nb, seq, ATT_KV_HEADS * HEAD_DIM), vt, seq)
    kmt, vm = _memkv(mem, g_mem[None, :], w_mem_kv.astype(BF16), g_k_mem[None, :])
    return _final(xp, xs, o_f, o_b, za, att.reshape(nb * seq, W_B), zb, qc, zc, gates, kmt, vm,
                  g_gdn_out[None, :], w_down_a.astype(BF16), w_down_b.astype(BF16), w_down_c.astype(BF16),
                  w_out.astype(BF16), seq)


def kernel(x_prompt, x_sample, mem_prompt, mem_sample, g_norm, g_mem, w_in, w_conv, a_log_f, a_log_b, dt_bias_f,
           dt_bias_b, g_gdn_out, g_q_attn, g_k_attn, g_q_mem, g_k_mem, w_mem_kv, w_down_a, w_down_b, w_down_c,
           w_out):
    bp, seq, _ = x_prompt.shape
    bs, seq_s, _ = x_sample.shape
    assert seq == seq_s and seq % TOK_BLOCK == 0 and seq % (2 * ATT_TK) == 0 and TOK_BLOCK == ATT_TK
    yp = x_prompt.reshape(bp * seq, D_MODEL)
    ys = x_sample.reshape(bs * seq, D_MODEL)
    mem = jnp.concatenate([mem_prompt, mem_sample], axis=0)
    for l in range(g_norm.shape[0]):
        yp, ys = _layer(yp, ys, mem, seq, g_norm[l], g_mem[l], w_in[l], w_conv[l], a_log_f[l], a_log_b[l],
                        dt_bias_f[l], dt_bias_b[l], g_gdn_out[l], g_q_attn[l], g_k_attn[l], g_q_mem[l],
                        g_k_mem[l], w_mem_kv[l], w_down_a[l], w_down_b[l], w_down_c[l], w_out[l])
    return (yp.reshape(bp, seq, D_MODEL), ys.reshape(bs, seq, D_MODEL))
```

```python
import functools
import math

import jax
import jax.numpy as jnp
from jax import lax
from jax.experimental import pallas as pl
from jax.experimental.pallas import tpu as pltpu

F32 = jnp.float32
BF16 = jnp.bfloat16

D_MODEL = 1024
GRID_W = 64
HEAD_DIM = 128
EPS = 1e-6
GDN_HEADS = 4
GDN_DK = 128
GDN_DV = 128
GDN_CONV = 5
GDN_CHUNK = 64
ATT_HEADS = 8
ATT_KV_HEADS = 2
ROPE_THETA = 10000.0
MEM_HEADS = 4
N_BRANCH = 3

W_A = GDN_HEADS * GDN_DV
W_B = ATT_HEADS * HEAD_DIM
W_C = MEM_HEADS * HEAD_DIM
N_QKV_A = 2 * GDN_HEADS * GDN_DK + GDN_HEADS * GDN_DV
N_SCAL_A = 4 * GDN_HEADS
N_KV_B = 2 * ATT_KV_HEADS * HEAD_DIM
N_GATE = N_BRANCH * D_MODEL

LANES = 128
LOG2E = math.log2(math.e)
SM_SCALE = HEAD_DIM ** -0.5 * LOG2E

OFF_QKVA = 0
OFF_ZA = OFF_QKVA + N_QKV_A
OFF_QB = OFF_ZA + W_A
OFF_KVB = OFF_QB + W_B
OFF_ZB = OFF_KVB + N_KV_B
OFF_QC = OFF_ZB + W_B
OFF_ZC = OFF_QC + W_C
OFF_GATE = OFF_ZC + W_C
OFF_SCAL = OFF_GATE + N_GATE
N_PACK = OFF_SCAL + LANES

TOK_BLOCK = 512
GDN_BLOCK = 256
ATT_TQ = 256
ATT_QBLOCK = 512
ATT_UNROLL = 4
ATT_TK = 512
N_SPLIT = 3
VMEM_LIMIT = 56 * 1024 * 1024


def _sigmoid(x):
    return 1.0 / (1.0 + jnp.exp(-x))


def _softplus(x):
    return jnp.maximum(x, 0.0) + jnp.log(1.0 + jnp.exp(-jnp.abs(x)))


def _rms_heads(y, g):
    ms = jnp.mean(y * y, axis=-1, keepdims=True)
    return y * lax.rsqrt(ms + EPS) * g


def _rope(y, cos, sin_signed):
    lane = lax.broadcasted_iota(jnp.int32, y.shape, 1)
    fwd = pltpu.roll(y, 32, axis=1)
    bwd = pltpu.roll(y, 96, axis=1)
    partner = jnp.where((lane & 32) == 0, bwd, fwd)
    return y * cos + partner * sin_signed


def _dot(a, b):
    return jnp.dot(a, b, preferred_element_type=F32)


def _inproj_kernel(xp_ref, xs_ref, gn_ref, w_ref, cos_ref, sin_ref, gq_ref, gk_ref, gqm_ref, alog_ref, dtb_ref,
                   qkva_ref, za_ref, qt_ref, k_ref, vt_ref, zb_ref, qc_ref, zc_ref, gates_ref,
                   scal_ref, sc3_ref, sct3_ref, h_scr, *, n_prompt_blocks):
    def norm_into_scratch(x_ref):
        x = x_ref[...]
        ms = jnp.mean(x * x, axis=-1, keepdims=True)
        h_scr[...] = (x * lax.rsqrt(ms + EPS) * gn_ref[...]).astype(BF16)

    @pl.when(pl.program_id(0) < n_prompt_blocks)
    def _():
        norm_into_scratch(xp_ref)

    @pl.when(pl.program_id(0) >= n_prompt_blocks)
    def _():
        norm_into_scratch(xs_ref)

    def mm(off, n):
        return _dot(h_scr[...], w_ref[:, off:off + n])

    step = 512
    for c in range(0, N_QKV_A, step):
        qkva_ref[:, c:c + step] = mm(OFF_QKVA + c, step).astype(BF16)

    z = mm(OFF_ZA, W_A)
    za_ref[...] = (z * _sigmoid(z)).astype(BF16)

    cos = cos_ref[...]
    sin = sin_ref[...]
    for c in range(0, W_B, step):
        y = mm(OFF_QB + c, step)
        for j in range(step // HEAD_DIM):
            yh = _rope(_rms_heads(y[:, j * HEAD_DIM:(j + 1) * HEAD_DIM], gq_ref[...]), cos, sin)
            qt_ref[0, c // HEAD_DIM + j] = (yh * SM_SCALE).T.astype(BF16)

    y = mm(OFF_KVB, N_KV_B)
    for g in range(ATT_KV_HEADS):
        kh = _rope(_rms_heads(y[:, g * HEAD_DIM:(g + 1) * HEAD_DIM], gk_ref[...]), cos, sin)
        k_ref[:, g * HEAD_DIM:(g + 1) * HEAD_DIM] = kh.astype(BF16)
        vh = y[:, (ATT_KV_HEADS + g) * HEAD_DIM:(ATT_KV_HEADS + g + 1) * HEAD_DIM]
        vt_ref[0, g, 0] = vh.T.astype(BF16)

    for c in range(0, W_B, step):
        z = mm(OFF_ZB + c, step)
        zb_ref[:, c:c + step] = (z * _sigmoid(z)).astype(BF16)

    y = mm(OFF_QC, W_C)
    for j in range(MEM_HEADS):
        yh = _rms_heads(y[:, j * HEAD_DIM:(j + 1) * HEAD_DIM], gqm_ref[...])
        qc_ref[:, j * HEAD_DIM:(j + 1) * HEAD_DIM] = (yh * SM_SCALE).astype(BF16)

    z = mm(OFF_ZC, W_C)
    zc_ref[...] = (z * _sigmoid(z)).astype(BF16)

    for c in range(0, N_GATE, step):
        gates_ref[:, c:c + step] = _sigmoid(mm(OFF_GATE + c, step)).astype(BF16)

    s = mm(OFF_SCAL, LANES)
    lane = lax.broadcasted_iota(jnp.int32, s.shape, 1)
    act = jnp.where(lane < 2 * GDN_HEADS, _sigmoid(s), -jnp.exp(alog_ref[...]) * _softplus(s + dtb_ref[...]))
    act = jnp.where(lane < N_SCAL_A, act, 0.0)
    scal_ref[...] = act[:, :N_SCAL_A]
    hi = act.astype(BF16).astype(F32)
    rem = act - hi
    mid = rem.astype(BF16).astype(F32)
    low = rem - mid
    packed = hi + pltpu.roll(mid, N_SCAL_A, axis=1) + pltpu.roll(low, 2 * N_SCAL_A, axis=1)
    sc3_ref[...] = packed.astype(BF16)
    sct3_ref[...] = packed.T[:N_SPLIT * N_SCAL_A, :].astype(BF16)


def _inproj(xp, xs, g_norm, w_pack, cos, sin, g_q, g_k, g_qm, alog, dtb, seq):
    tm = TOK_BLOCK
    npb = xp.shape[0] // tm
    ntok = xp.shape[0] + xs.shape[0]
    nblk = seq // tm
    nb = ntok // seq
    row = lambda i: (i, 0)
    const = lambda i: (0, 0)
    pos = lambda i: (i % nblk, 0)
    vec = pl.BlockSpec((1, LANES), const)
    out_shape = (
        jax.ShapeDtypeStruct((ntok, N_QKV_A), BF16),
        jax.ShapeDtypeStruct((ntok, W_A), BF16),
        jax.ShapeDtypeStruct((nb, ATT_HEADS, HEAD_DIM, seq), BF16),
        jax.ShapeDtypeStruct((ntok, ATT_KV_HEADS * HEAD_DIM), BF16),
        jax.ShapeDtypeStruct((nb, ATT_KV_HEADS, seq // ATT_TK, HEAD_DIM, ATT_TK), BF16),
        jax.ShapeDtypeStruct((ntok, W_B), BF16),
        jax.ShapeDtypeStruct((ntok, W_C), BF16),
        jax.ShapeDtypeStruct((ntok, W_C), BF16),
        jax.ShapeDtypeStruct((ntok, N_GATE), BF16),
        jax.ShapeDtypeStruct((ntok, N_SCAL_A), F32),
        jax.ShapeDtypeStruct((ntok, LANES), BF16),
        jax.ShapeDtypeStruct((N_SPLIT * N_SCAL_A, ntok), BF16),
    )
    out_specs = (
        pl.BlockSpec((tm, N_QKV_A), row),
        pl.BlockSpec((tm, W_A), row),
        pl.BlockSpec((1, ATT_HEADS, HEAD_DIM, tm), lambda i: (i // nblk, 0, 0, i % nblk)),
        pl.BlockSpec((tm, ATT_KV_HEADS * HEAD_DIM), row),
        pl.BlockSpec((1, ATT_KV_HEADS, 1, HEAD_DIM, ATT_TK), lambda i: (i // nblk, 0, i % nblk, 0, 0)),
        pl.BlockSpec((tm, W_B), row),
        pl.BlockSpec((tm, W_C), row),
        pl.BlockSpec((tm, W_C), row),
        pl.BlockSpec((tm, N_GATE), row),
        pl.BlockSpec((tm, N_SCAL_A), row),
        pl.BlockSpec((tm, LANES), row),
        pl.BlockSpec((N_SPLIT * N_SCAL_A, tm), lambda i: (0, i)),
    )
    in_specs = [
        pl.BlockSpec((tm, D_MODEL), lambda i: (jnp.minimum(i, npb - 1), 0)),
        pl.BlockSpec((tm, D_MODEL), lambda i: (jnp.maximum(i - npb, 0), 0)),
        pl.BlockSpec((1, D_MODEL), const),
        pl.BlockSpec((D_MODEL, N_PACK), const, pipeline_mode=pl.Buffered(1)),
        pl.BlockSpec((tm, LANES), pos),
        pl.BlockSpec((tm, LANES), pos),
        vec, vec, vec, vec, vec,
    ]
    return pl.pallas_call(
        functools.partial(_inproj_kernel, n_prompt_blocks=npb),
        grid=(ntok // tm,),
        in_specs=in_specs,
        out_specs=out_specs,
        out_shape=out_shape,
        scratch_shapes=[pltpu.VMEM((tm, D_MODEL), BF16)],
        compiler_params=pltpu.CompilerParams(dimension_semantics=("parallel",), vmem_limit_bytes=VMEM_LIMIT),
        name="inproj",
    )(xp, xs, g_norm, w_pack, cos, sin, g_q, g_k, g_qm, alog, dtb)


HALO = 8


def _gdn_prep_kernel(prev_ref, cur_ref, next_ref, wc_ref, q_ref, k_ref, v_ref, xs_scr, *, nblk):
    tm = cur_ref.shape[0]
    sblk = pl.program_id(0) % nblk
    xs_scr[HALO:HALO + tm, :] = cur_ref[...].astype(F32)
    xs_scr[0:HALO, :] = jnp.where(sblk == 0, 0.0, prev_ref[...].astype(F32))
    xs_scr[HALO + tm:2 * HALO + tm, :] = jnp.where(sblk == nblk - 1, 0.0, next_ref[...].astype(F32))
    pad = (GDN_CONV - 1) // 2
    nqk = GDN_HEADS * GDN_DK
    for c in range(0, N_QKV_A, LANES):
        acc = None
        for j in range(GDN_CONV):
            t = xs_scr[HALO - pad + j:HALO - pad + j + tm, c:c + LANES] * wc_ref[j:j + 1, c:c + LANES]
            acc = t if acc is None else acc + t
        y = acc * _sigmoid(acc)
        if c < 2 * nqk:
            y = y * lax.rsqrt(jnp.sum(y * y, axis=-1, keepdims=True) + EPS)
        if c < nqk:
            q_ref[:, c:c + LANES] = (y * (GDN_DK ** -0.5)).astype(BF16)
        elif c < 2 * nqk:
            k_ref[:, c - nqk:c - nqk + LANES] = y.astype(BF16)
        else:
            v_ref[:, c - 2 * nqk:c - 2 * nqk + LANES] = y.astype(BF16)


def _gdn_prep(qkva, w_conv, seq):
    ntok = qkva.shape[0]
    tm = TOK_BLOCK
    nblk = seq // tm
    hb = tm // HALO
    nhalo = ntok // HALO
    out = jax.ShapeDtypeStruct((ntok, W_A), BF16)
    row = lambda i: (i, 0)
    return pl.pallas_call(
        functools.partial(_gdn_prep_kernel, nblk=nblk),
        grid=(ntok // tm,),
        in_specs=[
            pl.BlockSpec((HALO, N_QKV_A), lambda i: (jnp.maximum(i * hb - 1, 0), 0)),
            pl.BlockSpec((tm, N_QKV_A), row),
            pl.BlockSpec((HALO, N_QKV_A), lambda i: (jnp.minimum((i + 1) * hb, nhalo - 1), 0)),
            pl.BlockSpec((GDN_CONV, N_QKV_A), lambda i: (0, 0)),
        ],
        out_specs=(pl.BlockSpec((tm, W_A), row),) * 3,
        out_shape=(out, out, out),
        scratch_shapes=[pltpu.VMEM((tm + 2 * HALO, N_QKV_A), F32)],
        compiler_params=pltpu.CompilerParams(dimension_semantics=("parallel",), vmem_limit_bytes=VMEM_LIMIT),
        name="gdn_prep",
    )(qkva, qkva, qkva, w_conv)


def _gdn_scan_kernel(qf_ref, kf_ref, vf_ref, scf_ref, sc3f_ref, sct3f_ref,
                     qb_ref, kb_ref, vb_ref, scb_ref, sc3b_ref, sct3b_ref, of_ref, ob_ref, state_ref):
    @pl.when(pl.program_id(1) == 0)
    def _():
        state_ref[...] = jnp.zeros_like(state_ref)

    n = GDN_BLOCK
    nchunk = n // GDN_CHUNK
    shift = GDN_CHUNK.bit_length() - 1
    row = lax.broadcasted_iota(jnp.int32, (n, n), 0)
    col = lax.broadcasted_iota(jnp.int32, (n, n), 1)
    same = (row >> shift) == (col >> shift)
    one = jnp.float32(1.0)
    zero = jnp.float32(0.0)
    rchunk = lax.broadcasted_iota(jnp.int32, (n, 2 * GDN_DV), 0) >> shift
    nsc = N_SCAL_A
    eye_w = jnp.where(lax.broadcasted_iota(jnp.int32, (GDN_CHUNK, n), 0)
                      == (lax.broadcasted_iota(jnp.int32, (GDN_CHUNK, n), 1) & (GDN_CHUNK - 1)), one, zero)

    def wide(x):
        out = x[0:GDN_CHUNK]
        for c in range(1, nchunk):
            out = out + x[c * GDN_CHUNK:(c + 1) * GDN_CHUNK]
        return out

    def block_diag(w):
        return jnp.where(same, jnp.concatenate([w] * nchunk, axis=0), 0.0).astype(BF16)

    probs = []
    for d, (q_ref, k_ref, v_ref, sc_ref, sc3_ref, sct3_ref, o_ref) in enumerate((
            (qf_ref, kf_ref, vf_ref, scf_ref, sc3f_ref, sct3f_ref, of_ref),
            (qb_ref, kb_ref, vb_ref, scb_ref, sc3b_ref, sct3b_ref, ob_ref))):
        if d == 1:
            incl = same & (row <= col)
            strict = same & (row < col)
            incl_t = same & (row >= col)
        else:
            incl = same & (row >= col)
            strict = same & (row > col)
            incl_t = same & (row <= col)
        m01 = jnp.concatenate([jnp.where(incl, one, zero), jnp.where(same, one, zero)], axis=0).astype(BF16)
        gg = _dot(m01, sc3_ref[...])
        gg = gg + pltpu.roll(gg, LANES - nsc, axis=1) + pltpu.roll(gg, LANES - 2 * nsc, axis=1)
        gr3 = _dot(sct3_ref[...], jnp.where(incl_t, one, zero).astype(BF16))
        g_row_all = gr3[0:nsc] + gr3[nsc:2 * nsc] + gr3[2 * nsc:3 * nsc]
        sc = sc_ref[...]
        for h in range(GDN_HEADS):
            cb = d * GDN_HEADS + h
            cg = 2 * GDN_HEADS + cb
            hs = slice(h * GDN_DK, (h + 1) * GDN_DK)
            probs.append(dict(
                idx=cb, h=h, reverse=d == 1, incl=incl, strict=strict, o_ref=o_ref,
                beta=sc[:, cb:cb + 1], gc=gg[:n, cg:cg + 1], gl=gg[n:, cg:cg + 1], gr=g_row_all[cg:cg + 1, :],
                q_ref=q_ref, k_ref=k_ref, v_ref=v_ref, hs=hs))

    for p in probs:
        incl = p["incl"]
        decay = jnp.where(incl, jnp.exp(jnp.where(incl, p["gc"] - p["gr"], 0.0)), 0.0)
        k_bf = p["k_ref"][:, p["hs"]]
        q_bf = p["q_ref"][:, p["hs"]]
        kb = k_bf.astype(F32) * p["beta"]
        lhs = jnp.concatenate([kb.astype(BF16), q_bf], axis=0)
        kq = lax.dot_general(lhs, k_bf, (((1,), (1,)), ((), ())), preferred_element_type=F32)
        a_neg = jnp.where(p["strict"], -(kq[:n] * decay), 0.0)
        p["qk"] = (kq[n:] * decay).astype(BF16)
        p["pd"] = a_neg.astype(BF16)
        p["pw"] = wide(a_neg)
        p["kb"] = kb

    for p in probs:
        p["tw"] = eye_w + p["pw"]
        p["pw"] = _dot(p["pw"].astype(BF16), p["pd"])
    for it in range(1, shift):
        for p in probs:
            p["pd"] = block_diag(p["pw"])
        last = it == shift - 1
        for p in probs:
            if last:
                p["tw"] = p["tw"] + _dot(p["tw"].astype(BF16), p["pd"])
            else:
                both = _dot(jnp.concatenate([p["tw"], p["pw"]], axis=0).astype(BF16), p["pd"])
                p["tw"] = p["tw"] + both[:GDN_CHUNK]
                p["pw"] = both[GDN_CHUNK:]
    for p in probs:
        p["t"] = block_diag(p["tw"])

    for p in probs:
        e_g = jnp.exp(p["gc"])
        v = p["v_ref"][:, p["hs"]].astype(F32)
        rhs = jnp.concatenate([(v * p["beta"]).astype(BF16), (p["kb"] * e_g).astype(BF16)], axis=1)
        p["uw"] = _dot(p["t"], rhs)
        p["e_g"] = e_g

    for p in probs:
        qkuw = _dot(p["qk"], p["uw"].astype(BF16))
        p["local"] = qkuw[:, :GDN_DV]
        q = p["q_ref"][:, p["hs"]].astype(F32)
        p["q_eff"] = (q * p["e_g"] - qkuw[:, GDN_DV:]).astype(BF16)
        k = p["k_ref"][:, p["hs"]].astype(F32)
        p["kdec_t"] = (k * jnp.exp(p["gl"] - p["gc"])).T.astype(BF16)
        p["e_gl"] = jnp.exp(p["gl"])

    for p in probs:
        p["kx"] = [_dot(p["kdec_t"], jnp.where(rchunk == c, p["uw"], 0.0).astype(BF16)) for c in range(nchunk)]
        p["state"] = state_ref[p["idx"]]

    for step in range(nchunk):
        for p in probs:
            c = nchunk - 1 - step if p["reverse"] else step
            r0 = c * GDN_CHUNK
            s_bf = p["state"].astype(BF16)
            o_c = _dot(p["q_eff"][r0:r0 + GDN_CHUNK], s_bf) + p["local"][r0:r0 + GDN_CHUNK]
            p["o_ref"][r0:r0 + GDN_CHUNK, p["hs"]] = o_c
            kx = p["kx"][c]
            p["state"] = (p["e_gl"][r0:r0 + 1, :] * p["state"] - _dot(kx[:, GDN_DV:].astype(BF16), s_bf)
                          + kx[:, :GDN_DV])

    for p in probs:
        state_ref[p["idx"]] = p["state"]


def _gdn_scan(q, k, v, scal, sc3, sct3, seq):
    ntok = q.shape[0]
    n = GDN_BLOCK
    nt = seq // n
    nb = ntok // seq
    fwd = lambda b, j: (b * nt + j, 0)
    bwd = lambda b, j: (b * nt + nt - 1 - j, 0)
    fwd_t = lambda b, j: (0, b * nt + j)
    bwd_t = lambda b, j: (0, b * nt + nt - 1 - j)
    tok = lambda m: pl.BlockSpec((n, W_A), m)
    side = lambda m, mt: [tok(m), tok(m), tok(m), pl.BlockSpec((n, N_SCAL_A), m), pl.BlockSpec((n, LANES), m),
                          pl.BlockSpec((N_SPLIT * N_SCAL_A, n), mt)]
    out = jax.ShapeDtypeStruct((ntok, W_A), F32)
    return pl.pallas_call(
        _gdn_scan_kernel,
        grid=(nb, nt),
        in_specs=side(fwd, fwd_t) + side(bwd, bwd_t),
        out_specs=(tok(fwd), tok(bwd)),
        out_shape=(out, out),
        scratch_shapes=[pltpu.VMEM((2 * GDN_HEADS, GDN_DK, GDN_DV), F32)],
        compiler_params=pltpu.CompilerParams(dimension_semantics=("parallel", "arbitrary"),
                                             vmem_limit_bytes=VMEM_LIMIT),
        name="gdn_scan",
    )(q, k, v, scal, sc3, sct3, q, k, v, scal, sc3, sct3)


def _attn_kernel(qt_ref, k_ref, vt_ref, o_ref, m_scr, l_scr, acc_scr, s_scr, mb_scr):
    grp = qt_ref.shape[1]
    tq = qt_ref.shape[3]
    nkv = vt_ref.shape[2]
    ntile = tq // ATT_TQ
    chains = [(h, t) for h in range(grp) for t in range(ntile)]
    m_scr[...] = jnp.full_like(m_scr, -jnp.inf)
    l_scr[...] = jnp.zeros_like(l_scr)
    acc_scr[...] = jnp.zeros_like(acc_scr)

    def scores(j, slot, c):
        h, t = chains[c]
        start = pl.multiple_of(j * ATT_TK, ATT_TK)
        s = _dot(k_ref[0, pl.ds(start, ATT_TK), :], qt_ref[0, h, :, t * ATT_TQ:(t + 1) * ATT_TQ])
        s_scr[slot, c] = s
        mb_scr[slot, c] = jnp.max(s, axis=0, keepdims=True)

    def update(j, slot, c):
        m_prev = m_scr[c]
        m_new = jnp.maximum(m_prev, mb_scr[slot, c])
        alpha = jnp.exp2(m_prev - m_new)
        p = jnp.exp2(s_scr[slot, c] - m_new)
        l_scr[c] = alpha * l_scr[c] + jnp.sum(p, axis=0, keepdims=True)
        acc_scr[c] = alpha * acc_scr[c] + _dot(vt_ref[0, 0, j], p.astype(BF16))
        m_scr[c] = m_new

    def stage(j, slot):
        j_next = jnp.minimum(j + 1, nkv - 1)
        for c in range(0, len(chains), 2):
            scores(j_next, 1 - slot, c)
            scores(j_next, 1 - slot, c + 1)
            update(j, slot, c)
            update(j, slot, c + 1)

    for c in range(len(chains)):
        scores(0, 0, c)

    def body(i, carry):
        for u in range(ATT_UNROLL):
            stage(ATT_UNROLL * i + u, u % 2)
        return carry

    lax.fori_loop(0, nkv // ATT_UNROLL, body, 0)
    for c, (h, t) in enumerate(chains):
        out = acc_scr[c] * (1.0 / l_scr[c])
        o_ref[0, t * ATT_TQ:(t + 1) * ATT_TQ, h * HEAD_DIM:(h + 1) * HEAD_DIM] = out.T.astype(o_ref.dtype)


def _attention(qt, k, vt, seq):
    nb = qt.shape[0]
    grp = ATT_HEADS // ATT_KV_HEADS
    tq = ATT_QBLOCK
    nchain = grp * (tq // ATT_TQ)
    return pl.pallas_call(
        _attn_kernel,
        grid=(nb, ATT_KV_HEADS, seq // tq),
        in_specs=[
            pl.BlockSpec((1, grp, HEAD_DIM, tq), lambda b, g, i: (b, g, 0, i)),
            pl.BlockSpec((1, seq, HEAD_DIM), lambda b, g, i: (b, 0, g)),
            pl.BlockSpec((1, 1, seq // ATT_TK, HEAD_DIM, ATT_TK), lambda b, g, i: (b, g, 0, 0, 0)),
        ],
        out_specs=pl.BlockSpec((1, tq, grp * HEAD_DIM), lambda b, g, i: (b, i, g)),
        out_shape=jax.ShapeDtypeStruct((nb, seq, W_B), BF16),
        scratch_shapes=[
            pltpu.VMEM((nchain, 1, ATT_TQ), F32),
            pltpu.VMEM((nchain, 1, ATT_TQ), F32),
            pltpu.VMEM((nchain, HEAD_DIM, ATT_TQ), F32),
            pltpu.VMEM((2, nchain, ATT_TK, ATT_TQ), F32),
            pltpu.VMEM((2, nchain, 1, ATT_TQ), F32),
        ],
        compiler_params=pltpu.CompilerParams(dimension_semantics=("parallel", "parallel", "arbitrary"),
                                             vmem_limit_bytes=VMEM_LIMIT),
        name="attention",
    )(qt, k, vt)


def _memkv_kernel(mem_ref, gm_ref, w_ref, gk_ref, kt_ref, v_ref):
    m = mem_ref[0]
    ms = jnp.mean(m * m, axis=-1, keepdims=True)
    mn = (m * lax.rsqrt(ms + EPS) * gm_ref[...]).astype(BF16)
    kv = _dot(mn, w_ref[...])
    for h in range(MEM_HEADS):
        kh = _rms_heads(kv[:, h * HEAD_DIM:(h + 1) * HEAD_DIM], gk_ref[...])
        kt_ref[0, h] = kh.T.astype(BF16)
    v_ref[0] = kv[:, W_C:].astype(BF16)


def _memkv(mem, g_mem, w_mem_kv, g_k_mem):
    nb, n_mem, _ = mem.shape
    return pl.pallas_call(
        _memkv_kernel,
        grid=(nb,),
        in_specs=[
            pl.BlockSpec((1, n_mem, D_MODEL), lambda b: (b, 0, 0)),
            pl.BlockSpec((1, D_MODEL), lambda b: (0, 0)),
            pl.BlockSpec((D_MODEL, 2 * W_C), lambda b: (0, 0)),
            pl.BlockSpec((1, HEAD_DIM), lambda b: (0, 0)),
        ],
        out_specs=(
            pl.BlockSpec((1, MEM_HEADS, HEAD_DIM, n_mem), lambda b: (b, 0, 0, 0)),
            pl.BlockSpec((1, n_mem, W_C), lambda b: (b, 0, 0)),
        ),
        out_shape=(
            jax.ShapeDtypeStruct((nb, MEM_HEADS, HEAD_DIM, n_mem), BF16),
            jax.ShapeDtypeStruct((nb, n_mem, W_C), BF16),
        ),
        compiler_params=pltpu.CompilerParams(dimension_semantics=("parallel",), vmem_limit_bytes=VMEM_LIMIT),
        name="memkv",
    )(mem, g_mem, w_mem_kv, g_k_mem)


def _final_kernel(xp_ref, xs_ref, of_ref, ob_ref, za_ref, att_ref, zb_ref, qc_ref, zc_ref, gates_ref, kmt_ref,
                  vm_ref, gout_ref, wda_ref, wdb_ref, wdc_ref, wout_ref, yp_ref, ys_ref, oa_scr, oc_scr,
                  *, n_prompt_blocks):
    o_gdn = of_ref[...] + ob_ref[...]
    for h in range(GDN_HEADS):
        hs = slice(h * GDN_DV, (h + 1) * GDN_DV)
        oa_scr[:, hs] = (_rms_heads(o_gdn[:, hs], gout_ref[...]) * za_ref[:, hs].astype(F32)).astype(BF16)
    ya = _dot(oa_scr[...], wda_ref[...])

    o_att = (att_ref[...].astype(F32) * zb_ref[...].astype(F32)).astype(BF16)
    yb = _dot(o_att, wdb_ref[...])

    for h in range(MEM_HEADS):
        hs = slice(h * HEAD_DIM, (h + 1) * HEAD_DIM)
        s = _dot(qc_ref[:, hs], kmt_ref[0, h])
        p = jnp.exp2(s - jnp.max(s, axis=-1, keepdims=True))
        l = jnp.sum(p, axis=-1, keepdims=True)
        o = _dot(p.astype(BF16), vm_ref[0, :, hs]) * (1.0 / l)
        oc_scr[:, hs] = (o * zc_ref[:, hs].astype(F32)).astype(BF16)
    yc = _dot(oc_scr[...], wdc_ref[...])

    mixed = (gates_ref[:, 0:D_MODEL].astype(F32) * ya
             + gates_ref[:, D_MODEL:2 * D_MODEL].astype(F32) * yb
             + gates_ref[:, 2 * D_MODEL:3 * D_MODEL].astype(F32) * yc)
    delta = _dot(mixed.astype(BF16), wout_ref[...])

    @pl.when(pl.program_id(0) < n_prompt_blocks)
    def _():
        yp_ref[...] = xp_ref[...] + delta

    @pl.when(pl.program_id(0) >= n_prompt_blocks)
    def _():
        ys_ref[...] = xs_ref[...] + delta


def _final(xp, xs, o_f, o_b, za, att, zb, qc, zc, gates, kmt, vm, g_out, wda, wdb, wdc, wout, seq):
    tm = TOK_BLOCK
    npb = xp.shape[0] // tm
    ntok = xp.shape[0] + xs.shape[0]
    nblk = seq // tm
    n_mem = vm.shape[1]
    row = lambda i: (i, 0)
    const = lambda i: (0, 0)
    prompt = lambda i: (jnp.minimum(i, npb - 1), 0)
    sample = lambda i: (jnp.maximum(i - npb, 0), 0)
    tokspec = lambda w: pl.BlockSpec((tm, w), row)
    return pl.pallas_call(
        functools.partial(_final_kernel, n_prompt_blocks=npb),
        grid=(ntok // tm,),
        in_specs=[
            pl.BlockSpec((tm, D_MODEL), prompt), pl.BlockSpec((tm, D_MODEL), sample),
            tokspec(W_A), tokspec(W_A), tokspec(W_A), tokspec(W_B), tokspec(W_B),
            tokspec(W_C), tokspec(W_C), tokspec(N_GATE),
            pl.BlockSpec((1, MEM_HEADS, HEAD_DIM, n_mem), lambda i: (i // nblk, 0, 0, 0)),
            pl.BlockSpec((1, n_mem, W_C), lambda i: (i // nblk, 0, 0)),
            pl.BlockSpec((1, GDN_DV), const),
            pl.BlockSpec((W_A, D_MODEL), const),
            pl.BlockSpec((W_B, D_MODEL), const),
            pl.BlockSpec((W_C, D_MODEL), const),
            pl.BlockSpec((D_MODEL, D_MODEL), const),
        ],
        out_specs=(pl.BlockSpec((tm, D_MODEL), prompt), pl.BlockSpec((tm, D_MODEL), sample)),
        out_shape=(jax.ShapeDtypeStruct(xp.shape, F32), jax.ShapeDtypeStruct(xs.shape, F32)),
        scratch_shapes=[pltpu.VMEM((tm, W_A), BF16), pltpu.VMEM((tm, W_C), BF16)],
        compiler_params=pltpu.CompilerParams(dimension_semantics=("arbitrary",), vmem_limit_bytes=VMEM_LIMIT),
        name="final",
    )(xp, xs, o_f, o_b, za, att, zb, qc, zc, gates, kmt, vm, g_out, wda, wdb, wdc, wout)


def _rope_tables(seq):
    half = HEAD_DIM // 2
    freqs = ROPE_THETA ** (-jnp.arange(0, half, 2, dtype=F32) / half)
    t = jnp.arange(seq, dtype=jnp.int32)
    ang_r = (t // GRID_W).astype(F32)[:, None] * freqs[None, :]
    ang_c = (t % GRID_W).astype(F32)[:, None] * freqs[None, :]
    cos = jnp.concatenate([jnp.cos(ang_r)] * 2 + [jnp.cos(ang_c)] * 2, axis=-1)
    sin = jnp.concatenate([-jnp.sin(ang_r), jnp.sin(ang_r), -jnp.sin(ang_c), jnp.sin(ang_c)], axis=-1)
    return cos, sin


def _pad_lanes(v):
    return jnp.pad(v.astype(F32), (0, LANES - v.shape[0]))[None, :]


def _layer(xp, xs, mem, seq, g_norm, g_mem, w_in, w_conv, a_log_f, a_log_b, dt_bias_f, dt_bias_b, g_gdn_out,
           g_q_attn, g_k_attn, g_q_mem, g_k_mem, w_mem_kv, w_down_a, w_down_b, w_down_c, w_out):
    o_s = N_QKV_A + W_A
    w_pack = jnp.concatenate(
        [w_in[:, :o_s], w_in[:, o_s + N_SCAL_A:], w_in[:, o_s:o_s + N_SCAL_A],
         jnp.zeros((D_MODEL, LANES - N_SCAL_A), w_in.dtype)], axis=1).astype(BF16)
    zeros8 = jnp.zeros((2 * GDN_HEADS,), F32)
    alog = _pad_lanes(jnp.concatenate([zeros8, a_log_f, a_log_b]))
    dtb = _pad_lanes(jnp.concatenate([zeros8, dt_bias_f, dt_bias_b]))
    cos, sin = _rope_tables(seq)

    (qkva, za, qt, k_att, vt, zb, qc, zc, gates, scal, sc3, sct3) = _inproj(
        xp, xs, g_norm[None, :], w_pack, cos, sin, g_q_attn[None, :], g_k_attn[None, :], g_q_mem[None, :], alog,
        dtb, seq)
    nb = qt.shape[0]
    q_a, k_a, v_a = _gdn_prep(qkva, w_conv, seq)
    o_f, o_b = _gdn_scan(q_a, k_a, v_a, scal, sc3, sct3, seq)
    att = _attention(qt, k_att.reshape(nb, seq, ATT_KV_HEADS * HEAD_DIM), vt, seq)
    kmt, vm = _memkv(mem, g_mem[None, :], w_mem_kv.astype(BF16), g_k_mem[None, :])
    return _final(xp, xs, o_f, o_b, za, att.reshape(nb * seq, W_B), zb, qc, zc, gates, kmt, vm,
                  g_gdn_out[None, :], w_down_a.astype(BF16), w_down_b.astype(BF16), w_down_c.astype(BF16),
                  w_out.astype(BF16), seq)


def kernel(x_prompt, x_sample, mem_prompt, mem_sample, g_norm, g_mem, w_in, w_conv, a_log_f, a_log_b, dt_bias_f,
           dt_bias_b, g_gdn_out, g_q_attn, g_k_attn, g_q_mem, g_k_mem, w_mem_kv, w_down_a, w_down_b, w_down_c,
           w_out):
    bp, seq, _ = x_prompt.shape
    bs, seq_s, _ = x_sample.shape
    assert seq == seq_s and seq % TOK_BLOCK == 0 and TOK_BLOCK == ATT_TK
    assert seq % (ATT_UNROLL * ATT_TK) == 0 and seq % ATT_QBLOCK == 0 and ATT_UNROLL % 2 == 0
    yp = x_prompt.reshape(bp * seq, D_MODEL)
    ys = x_sample.reshape(bs * seq, D_MODEL)
    mem = jnp.concatenate([mem_prompt, mem_sample], axis=0)
    for l in range(g_norm.shape[0]):
        yp, ys = _layer(yp, ys, mem, seq, g_norm[l], g_mem[l], w_in[l], w_conv[l], a_log_f[l], a_log_b[l],
                        dt_bias_f[l], dt_bias_b[l], g_gdn_out[l], g_q_attn[l], g_k_attn[l], g_q_mem[l],
                        g_k_mem[l], w_mem_kv[l], w_down_a[l], w_down_b[l], w_down_c[l], w_out[l])
    return (yp.reshape(bp, seq, D_MODEL), ys.reshape(bs, seq, D_MODEL))
```

```python
import functools
import math

import jax
import jax.numpy as jnp
from jax import lax
from jax.experimental import pallas as pl
from jax.experimental.pallas import tpu as pltpu

F32 = jnp.float32
BF16 = jnp.bfloat16

D_MODEL = 1024
GRID_W = 64
HEAD_DIM = 128
EPS = 1e-6
GDN_HEADS = 4
GDN_DK = 128
GDN_DV = 128
GDN_CONV = 5
GDN_CHUNK = 64
ATT_HEADS = 8
ATT_KV_HEADS = 2
ROPE_THETA = 10000.0
MEM_HEADS = 4
N_BRANCH = 3

W_A = GDN_HEADS * GDN_DV
W_B = ATT_HEADS * HEAD_DIM
W_C = MEM_HEADS * HEAD_DIM
N_QKV_A = 2 * GDN_HEADS * GDN_DK + GDN_HEADS * GDN_DV
N_SCAL_A = 4 * GDN_HEADS
N_KV_B = 2 * ATT_KV_HEADS * HEAD_DIM
N_GATE = N_BRANCH * D_MODEL

LANES = 128
LOG2E = math.log2(math.e)
SM_SCALE = HEAD_DIM ** -0.5 * LOG2E

OFF_QKVA = 0
OFF_ZA = OFF_QKVA + N_QKV_A
OFF_QB = OFF_ZA + W_A
OFF_KVB = OFF_QB + W_B
OFF_ZB = OFF_KVB + N_KV_B
OFF_QC = OFF_ZB + W_B
OFF_ZC = OFF_QC + W_C
OFF_GATE = OFF_ZC + W_C
OFF_SCAL = OFF_GATE + N_GATE

TOK_BLOCK = 512
GDN_BLOCK = 256
ATT_TQ = 256
ATT_QBLOCK = 512
ATT_UNROLL = 8
ATT_TK = 512
N_SPLIT = 3
VMEM_LIMIT = 56 * 1024 * 1024


def _sigmoid(x):
    return 1.0 / (1.0 + jnp.exp(-x))


def _softplus(x):
    return jnp.maximum(x, 0.0) + jnp.log(1.0 + jnp.exp(-jnp.abs(x)))


def _rms_heads(y, g):
    ms = jnp.mean(y * y, axis=-1, keepdims=True)
    return y * lax.rsqrt(ms + EPS) * g


def _rope(y, cos, sin_signed):
    lane = lax.broadcasted_iota(jnp.int32, y.shape, 1)
    fwd = pltpu.roll(y, 32, axis=1)
    bwd = pltpu.roll(y, 96, axis=1)
    partner = jnp.where((lane & 32) == 0, bwd, fwd)
    return y * cos + partner * sin_signed


def _dot(a, b):
    return jnp.dot(a, b, preferred_element_type=F32)


def _inproj_kernel(xp_ref, xs_ref, gn_ref, wa_ref, wb_ref, ws_ref, cos_ref, sin_ref, gq_ref, gk_ref, gqm_ref,
                   alog_ref, dtb_ref, qkva_ref, za_ref, qt_ref, k_ref, vt_ref, zb_ref, qc_ref, zc_ref, gates_ref,
                   scal_ref, sc3_ref, sct3_ref, h_scr, *, n_prompt_blocks):
    def norm_into_scratch(x_ref):
        x = x_ref[...]
        ms = jnp.mean(x * x, axis=-1, keepdims=True)
        h_scr[...] = (x * lax.rsqrt(ms + EPS) * gn_ref[...]).astype(BF16)

    @pl.when(pl.program_id(0) < n_prompt_blocks)
    def _():
        norm_into_scratch(xp_ref)

    @pl.when(pl.program_id(0) >= n_prompt_blocks)
    def _():
        norm_into_scratch(xs_ref)

    def mm(off, n):
        if off == OFF_SCAL:
            w = ws_ref[...]
        elif off < OFF_QB:
            w = wa_ref[:, off:off + n]
        else:
            w = wb_ref[:, off - OFF_QB:off - OFF_QB + n]
        return _dot(h_scr[...], w)

    step = 512
    for c in range(0, N_QKV_A, step):
        qkva_ref[:, c:c + step] = mm(OFF_QKVA + c, step).astype(BF16)

    z = mm(OFF_ZA, W_A)
    za_ref[...] = (z * _sigmoid(z)).astype(BF16)

    cos = cos_ref[...]
    sin = sin_ref[...]
    for c in range(0, W_B, step):
        y = mm(OFF_QB + c, step)
        for j in range(step // HEAD_DIM):
            yh = _rope(_rms_heads(y[:, j * HEAD_DIM:(j + 1) * HEAD_DIM], gq_ref[...]), cos, sin)
            qt_ref[0, c // HEAD_DIM + j] = (yh * SM_SCALE).T.astype(BF16)

    y = mm(OFF_KVB, N_KV_B)
    for g in range(ATT_KV_HEADS):
        kh = _rope(_rms_heads(y[:, g * HEAD_DIM:(g + 1) * HEAD_DIM], gk_ref[...]), cos, sin)
        k_ref[:, g * HEAD_DIM:(g + 1) * HEAD_DIM] = kh.astype(BF16)
        vh = y[:, (ATT_KV_HEADS + g) * HEAD_DIM:(ATT_KV_HEADS + g + 1) * HEAD_DIM]
        vt_ref[0, g, 0] = vh.T.astype(BF16)

    for c in range(0, W_B, step):
        z = mm(OFF_ZB + c, step)
        zb_ref[:, c:c + step] = (z * _sigmoid(z)).astype(BF16)

    y = mm(OFF_QC, W_C)
    for j in range(MEM_HEADS):
        yh = _rms_heads(y[:, j * HEAD_DIM:(j + 1) * HEAD_DIM], gqm_ref[...])
        qc_ref[:, j * HEAD_DIM:(j + 1) * HEAD_DIM] = (yh * SM_SCALE).astype(BF16)

    z = mm(OFF_ZC, W_C)
    zc_ref[...] = (z * _sigmoid(z)).astype(BF16)

    for c in range(0, N_GATE, step):
        gates_ref[:, c:c + step] = _sigmoid(mm(OFF_GATE + c, step)).astype(BF16)

    s = mm(OFF_SCAL, LANES)
    lane = lax.broadcasted_iota(jnp.int32, s.shape, 1)
    act = jnp.where(lane < 2 * GDN_HEADS, _sigmoid(s), -jnp.exp(alog_ref[...]) * _softplus(s + dtb_ref[...]))
    act = jnp.where(lane < N_SCAL_A, act, 0.0)
    scal_ref[...] = act[:, :N_SCAL_A]
    hi = act.astype(BF16).astype(F32)
    rem = act - hi
    mid = rem.astype(BF16).astype(F32)
    low = rem - mid
    packed = hi + pltpu.roll(mid, N_SCAL_A, axis=1) + pltpu.roll(low, 2 * N_SCAL_A, axis=1)
    sc3_ref[...] = packed.astype(BF16)
    sct3_ref[...] = packed.T[:N_SPLIT * N_SCAL_A, :].astype(BF16)


def _inproj(xp, xs, g_norm, w_a, w_b, w_s, cos, sin, g_q, g_k, g_qm, alog, dtb, seq):
    tm = TOK_BLOCK
    npb = xp.shape[0] // tm
    ntok = xp.shape[0] + xs.shape[0]
    nblk = seq // tm
    nb = ntok // seq
    row = lambda i: (i, 0)
    const = lambda i: (0, 0)
    pos = lambda i: (i % nblk, 0)
    vec = pl.BlockSpec((1, LANES), const)
    out_shape = (
        jax.ShapeDtypeStruct((ntok, N_QKV_A), BF16),
        jax.ShapeDtypeStruct((ntok, W_A), BF16),
        jax.ShapeDtypeStruct((nb, ATT_HEADS, HEAD_DIM, seq), BF16),
        jax.ShapeDtypeStruct((ntok, ATT_KV_HEADS * HEAD_DIM), BF16),
        jax.ShapeDtypeStruct((nb, ATT_KV_HEADS, seq // ATT_TK, HEAD_DIM, ATT_TK), BF16),
        jax.ShapeDtypeStruct((ntok, W_B), BF16),
        jax.ShapeDtypeStruct((ntok, W_C), BF16),
        jax.ShapeDtypeStruct((ntok, W_C), BF16),
        jax.ShapeDtypeStruct((ntok, N_GATE), BF16),
        jax.ShapeDtypeStruct((ntok, N_SCAL_A), F32),
        jax.ShapeDtypeStruct((ntok, LANES), BF16),
        jax.ShapeDtypeStruct((N_SPLIT * N_SCAL_A, ntok), BF16),
    )
    out_specs = (
        pl.BlockSpec((tm, N_QKV_A), row),
        pl.BlockSpec((tm, W_A), row),
        pl.BlockSpec((1, ATT_HEADS, HEAD_DIM, tm), lambda i: (i // nblk, 0, 0, i % nblk)),
        pl.BlockSpec((tm, ATT_KV_HEADS * HEAD_DIM), row),
        pl.BlockSpec((1, ATT_KV_HEADS, 1, HEAD_DIM, ATT_TK), lambda i: (i // nblk, 0, i % nblk, 0, 0)),
        pl.BlockSpec((tm, W_B), row),
        pl.BlockSpec((tm, W_C), row),
        pl.BlockSpec((tm, W_C), row),
        pl.BlockSpec((tm, N_GATE), row),
        pl.BlockSpec((tm, N_SCAL_A), row),
        pl.BlockSpec((tm, LANES), row),
        pl.BlockSpec((N_SPLIT * N_SCAL_A, tm), lambda i: (0, i)),
    )
    in_specs = [
        pl.BlockSpec((tm, D_MODEL), lambda i: (jnp.minimum(i, npb - 1), 0)),
        pl.BlockSpec((tm, D_MODEL), lambda i: (jnp.maximum(i - npb, 0), 0)),
        pl.BlockSpec((1, D_MODEL), const),
        pl.BlockSpec((D_MODEL, OFF_QB), const, pipeline_mode=pl.Buffered(1)),
        pl.BlockSpec((D_MODEL, OFF_SCAL - OFF_QB), const, pipeline_mode=pl.Buffered(1)),
        pl.BlockSpec((D_MODEL, LANES), const, pipeline_mode=pl.Buffered(1)),
        pl.BlockSpec((tm, LANES), pos),
        pl.BlockSpec((tm, LANES), pos),
        vec, vec, vec, vec, vec,
    ]
    return pl.pallas_call(
        functools.partial(_inproj_kernel, n_prompt_blocks=npb),
        grid=(ntok // tm,),
        in_specs=in_specs,
        out_specs=out_specs,
        out_shape=out_shape,
        scratch_shapes=[pltpu.VMEM((tm, D_MODEL), BF16)],
        compiler_params=pltpu.CompilerParams(dimension_semantics=("parallel",), vmem_limit_bytes=VMEM_LIMIT),
        name="inproj",
    )(xp, xs, g_norm, w_a, w_b, w_s, cos, sin, g_q, g_k, g_qm, alog, dtb)


HALO = 8


def _gdn_prep_kernel(prev_ref, cur_ref, next_ref, wc_ref, q_ref, k_ref, v_ref, xs_scr, *, nblk):
    tm = cur_ref.shape[0]
    sblk = pl.program_id(0) % nblk
    xs_scr[HALO:HALO + tm, :] = cur_ref[...].astype(F32)
    xs_scr[0:HALO, :] = jnp.where(sblk == 0, 0.0, prev_ref[...].astype(F32))
    xs_scr[HALO + tm:2 * HALO + tm, :] = jnp.where(sblk == nblk - 1, 0.0, next_ref[...].astype(F32))
    pad = (GDN_CONV - 1) // 2
    nqk = GDN_HEADS * GDN_DK
    for c in range(0, N_QKV_A, LANES):
        acc = None
        for j in range(GDN_CONV):
            t = xs_scr[HALO - pad + j:HALO - pad + j + tm, c:c + LANES] * wc_ref[j:j + 1, c:c + LANES]
            acc = t if acc is None else acc + t
        y = acc * _sigmoid(acc)
        if c < 2 * nqk:
            y = y * lax.rsqrt(jnp.sum(y * y, axis=-1, keepdims=True) + EPS)
        if c < nqk:
            q_ref[:, c:c + LANES] = (y * (GDN_DK ** -0.5)).astype(BF16)
        elif c < 2 * nqk:
            k_ref[:, c - nqk:c - nqk + LANES] = y.astype(BF16)
        else:
            v_ref[:, c - 2 * nqk:c - 2 * nqk + LANES] = y.astype(BF16)


def _gdn_prep(qkva, w_conv, seq):
    ntok = qkva.shape[0]
    tm = TOK_BLOCK
    nblk = seq // tm
    hb = tm // HALO
    nhalo = ntok // HALO
    out = jax.ShapeDtypeStruct((ntok, W_A), BF16)
    row = lambda i: (i, 0)
    return pl.pallas_call(
        functools.partial(_gdn_prep_kernel, nblk=nblk),
        grid=(ntok // tm,),
        in_specs=[
            pl.BlockSpec((HALO, N_QKV_A), lambda i: (jnp.maximum(i * hb - 1, 0), 0)),
            pl.BlockSpec((tm, N_QKV_A), row),
            pl.BlockSpec((HALO, N_QKV_A), lambda i: (jnp.minimum((i + 1) * hb, nhalo - 1), 0)),
            pl.BlockSpec((GDN_CONV, N_QKV_A), lambda i: (0, 0)),
        ],
        out_specs=(pl.BlockSpec((tm, W_A), row),) * 3,
        out_shape=(out, out, out),
        scratch_shapes=[pltpu.VMEM((tm + 2 * HALO, N_QKV_A), F32)],
        compiler_params=pltpu.CompilerParams(dimension_semantics=("parallel",), vmem_limit_bytes=VMEM_LIMIT),
        name="gdn_prep",
    )(qkva, qkva, qkva, w_conv)


def _gdn_scan_kernel(qf_ref, kf_ref, vf_ref, scf_ref, sc3f_ref, sct3f_ref,
                     qb_ref, kb_ref, vb_ref, scb_ref, sc3b_ref, sct3b_ref, of_ref, ob_ref, state_ref):
    @pl.when(pl.program_id(1) == 0)
    def _():
        state_ref[...] = jnp.zeros_like(state_ref)

    n = GDN_BLOCK
    nchunk = n // GDN_CHUNK
    shift = GDN_CHUNK.bit_length() - 1
    row = lax.broadcasted_iota(jnp.int32, (n, n), 0)
    col = lax.broadcasted_iota(jnp.int32, (n, n), 1)
    same = (row >> shift) == (col >> shift)
    one = jnp.float32(1.0)
    zero = jnp.float32(0.0)
    rchunk = lax.broadcasted_iota(jnp.int32, (n, 2 * GDN_DV), 0) >> shift
    nsc = N_SCAL_A
    eye_w = jnp.where(lax.broadcasted_iota(jnp.int32, (GDN_CHUNK, n), 0)
                      == (lax.broadcasted_iota(jnp.int32, (GDN_CHUNK, n), 1) & (GDN_CHUNK - 1)), one, zero)

    def wide(x):
        out = x[0:GDN_CHUNK]
        for c in range(1, nchunk):
            out = out + x[c * GDN_CHUNK:(c + 1) * GDN_CHUNK]
        return out

    def block_diag(w):
        return jnp.where(same, jnp.concatenate([w] * nchunk, axis=0), 0.0).astype(BF16)

    probs = []
    for d, (q_ref, k_ref, v_ref, sc_ref, sc3_ref, sct3_ref, o_ref) in enumerate((
            (qf_ref, kf_ref, vf_ref, scf_ref, sc3f_ref, sct3f_ref, of_ref),
            (qb_ref, kb_ref, vb_ref, scb_ref, sc3b_ref, sct3b_ref, ob_ref))):
        if d == 1:
            incl = same & (row <= col)
            strict = same & (row < col)
            incl_t = same & (row >= col)
        else:
            incl = same & (row >= col)
            strict = same & (row > col)
            incl_t = same & (row <= col)
        m01 = jnp.concatenate([jnp.where(incl, one, zero), jnp.where(same, one, zero)], axis=0).astype(BF16)
        gg = _dot(m01, sc3_ref[...])
        gg = gg + pltpu.roll(gg, LANES - nsc, axis=1) + pltpu.roll(gg, LANES - 2 * nsc, axis=1)
        gr3 = _dot(sct3_ref[...], jnp.where(incl_t, one, zero).astype(BF16))
        g_row_all = gr3[0:nsc] + gr3[nsc:2 * nsc] + gr3[2 * nsc:3 * nsc]
        sc = sc_ref[...]
        for h in range(GDN_HEADS):
            cb = d * GDN_HEADS + h
            cg = 2 * GDN_HEADS + cb
            hs = slice(h * GDN_DK, (h + 1) * GDN_DK)
            probs.append(dict(
                idx=cb, h=h, reverse=d == 1, incl=incl, strict=strict, o_ref=o_ref,
                beta=sc[:, cb:cb + 1], gc=gg[:n, cg:cg + 1], gl=gg[n:, cg:cg + 1], gr=g_row_all[cg:cg + 1, :],
                q_ref=q_ref, k_ref=k_ref, v_ref=v_ref, hs=hs))

    for p in probs:
        incl = p["incl"]
        decay = jnp.where(incl, jnp.exp(jnp.where(incl, p["gc"] - p["gr"], 0.0)), 0.0)
        k_bf = p["k_ref"][:, p["hs"]]
        q_bf = p["q_ref"][:, p["hs"]]
        kb = k_bf.astype(F32) * p["beta"]
        lhs = jnp.concatenate([kb.astype(BF16), q_bf], axis=0)
        kq = lax.dot_general(lhs, k_bf, (((1,), (1,)), ((), ())), preferred_element_type=F32)
        a_neg = jnp.where(p["strict"], -(kq[:n] * decay), 0.0)
        p["qk"] = (kq[n:] * decay).astype(BF16)
        p["pd"] = a_neg.astype(BF16)
        p["pw"] = wide(a_neg)
        p["kb"] = kb

    for p in probs:
        p["tw"] = eye_w + p["pw"]
        p["pw"] = _dot(p["pw"].astype(BF16), p["pd"])
    for it in range(1, shift):
        for p in probs:
            p["pd"] = block_diag(p["pw"])
        last = it == shift - 1
        for p in probs:
            if last:
                p["tw"] = p["tw"] + _dot(p["tw"].astype(BF16), p["pd"])
            else:
                both = _dot(jnp.concatenate([p["tw"], p["pw"]], axis=0).astype(BF16), p["pd"])
                p["tw"] = p["tw"] + both[:GDN_CHUNK]
                p["pw"] = both[GDN_CHUNK:]
    for p in probs:
        p["t"] = block_diag(p["tw"])

    for p in probs:
        e_g = jnp.exp(p["gc"])
        v = p["v_ref"][:, p["hs"]].astype(F32)
        rhs = jnp.concatenate([(v * p["beta"]).astype(BF16), (p["kb"] * e_g).astype(BF16)], axis=1)
        p["uw"] = _dot(p["t"], rhs)
        p["e_g"] = e_g

    for p in probs:
        qkuw = _dot(p["qk"], p["uw"].astype(BF16))
        p["local"] = qkuw[:, :GDN_DV]
        q = p["q_ref"][:, p["hs"]].astype(F32)
        p["q_eff"] = (q * p["e_g"] - qkuw[:, GDN_DV:]).astype(BF16)
        k = p["k_ref"][:, p["hs"]].astype(F32)
        p["kdec_t"] = (k * jnp.exp(p["gl"] - p["gc"])).T.astype(BF16)
        p["e_gl"] = jnp.exp(p["gl"])

    for p in probs:
        p["kx"] = [_dot(p["kdec_t"], jnp.where(rchunk == c, p["uw"], 0.0).astype(BF16)) for c in range(nchunk)]
        p["state"] = state_ref[p["idx"]]

    for step in range(nchunk):
        for p in probs:
            c = nchunk - 1 - step if p["reverse"] else step
            r0 = c * GDN_CHUNK
            s_bf = p["state"].astype(BF16)
            o_c = _dot(p["q_eff"][r0:r0 + GDN_CHUNK], s_bf) + p["local"][r0:r0 + GDN_CHUNK]
            p["o_ref"][r0:r0 + GDN_CHUNK, p["hs"]] = o_c
            kx = p["kx"][c]
            p["state"] = (p["e_gl"][r0:r0 + 1, :] * p["state"] - _dot(kx[:, GDN_DV:].astype(BF16), s_bf)
                          + kx[:, :GDN_DV])

    for p in probs:
        state_ref[p["idx"]] = p["state"]


def _gdn_scan(q, k, v, scal, sc3, sct3, seq):
    ntok = q.shape[0]
    n = GDN_BLOCK
    nt = seq // n
    nb = ntok // seq
    fwd = lambda b, j: (b * nt + j, 0)
    bwd = lambda b, j: (b * nt + nt - 1 - j, 0)
    fwd_t = lambda b, j: (0, b * nt + j)
    bwd_t = lambda b, j: (0, b * nt + nt - 1 - j)
    tok = lambda m: pl.BlockSpec((n, W_A), m)
    side = lambda m, mt: [tok(m), tok(m), tok(m), pl.BlockSpec((n, N_SCAL_A), m), pl.BlockSpec((n, LANES), m),
                          pl.BlockSpec((N_SPLIT * N_SCAL_A, n), mt)]
    out = jax.ShapeDtypeStruct((ntok, W_A), F32)
    return pl.pallas_call(
        _gdn_scan_kernel,
        grid=(nb, nt),
        in_specs=side(fwd, fwd_t) + side(bwd, bwd_t),
        out_specs=(tok(fwd), tok(bwd)),
        out_shape=(out, out),
        scratch_shapes=[pltpu.VMEM((2 * GDN_HEADS, GDN_DK, GDN_DV), F32)],
        compiler_params=pltpu.CompilerParams(dimension_semantics=("parallel", "arbitrary"),
                                             vmem_limit_bytes=VMEM_LIMIT),
        name="gdn_scan",
    )(q, k, v, scal, sc3, sct3, q, k, v, scal, sc3, sct3)


def _attn_kernel(qt_ref, k_ref, vt_ref, o_ref, m_scr, l_scr, acc_scr, s_scr, mb_scr):
    grp = qt_ref.shape[1]
    tq = qt_ref.shape[3]
    nkv = vt_ref.shape[2]
    ntile = tq // ATT_TQ
    chains = [(h, t) for h in range(grp) for t in range(ntile)]
    m_scr[...] = jnp.full_like(m_scr, -jnp.inf)
    l_scr[...] = jnp.zeros_like(l_scr)
    acc_scr[...] = jnp.zeros_like(acc_scr)

    def scores(j, slot, c):
        h, t = chains[c]
        start = pl.multiple_of(j * ATT_TK, ATT_TK)
        s = _dot(k_ref[0, pl.ds(start, ATT_TK), :], qt_ref[0, h, :, t * ATT_TQ:(t + 1) * ATT_TQ])
        s_scr[slot, c] = s
        mb_scr[slot, c] = jnp.max(s, axis=0, keepdims=True)

    def update(j, slot, c):
        m_prev = m_scr[c]
        m_new = jnp.maximum(m_prev, mb_scr[slot, c])
        alpha = jnp.exp2(m_prev - m_new)
        p = jnp.exp2(s_scr[slot, c] - m_new)
        l_scr[c] = alpha * l_scr[c] + jnp.sum(p, axis=0, keepdims=True)
        acc_scr[c] = alpha * acc_scr[c] + _dot(vt_ref[0, 0, j], p.astype(BF16))
        m_scr[c] = m_new

    def stage(j, slot):
        j_next = jnp.minimum(j + 1, nkv - 1)
        for c in range(len(chains)):
            scores(j_next, 1 - slot, c)
            update(j, slot, c)

    for c in range(len(chains)):
        scores(0, 0, c)

    def body(i, carry):
        for u in range(ATT_UNROLL):
            stage(ATT_UNROLL * i + u, u % 2)
        return carry

    lax.fori_loop(0, nkv // ATT_UNROLL, body, 0)
    for c, (h, t) in enumerate(chains):
        out = acc_scr[c] * (1.0 / l_scr[c])
        o_ref[0, t * ATT_TQ:(t + 1) * ATT_TQ, h * HEAD_DIM:(h + 1) * HEAD_DIM] = out.T.astype(o_ref.dtype)


def _attention(qt, k, vt, seq):
    nb = qt.shape[0]
    grp = ATT_HEADS // ATT_KV_HEADS
    tq = ATT_QBLOCK
    nchain = grp * (tq // ATT_TQ)
    return pl.pallas_call(
        _attn_kernel,
        grid=(nb, ATT_KV_HEADS, seq // tq),
        in_specs=[
            pl.BlockSpec((1, grp, HEAD_DIM, tq), lambda b, g, i: (b, g, 0, i)),
            pl.BlockSpec((1, seq, HEAD_DIM), lambda b, g, i: (b, 0, g)),
            pl.BlockSpec((1, 1, seq // ATT_TK, HEAD_DIM, ATT_TK), lambda b, g, i: (b, g, 0, 0, 0)),
        ],
        out_specs=pl.BlockSpec((1, tq, grp * HEAD_DIM), lambda b, g, i: (b, i, g)),
        out_shape=jax.ShapeDtypeStruct((nb, seq, W_B), BF16),
        scratch_shapes=[
            pltpu.VMEM((nchain, 1, ATT_TQ), F32),
            pltpu.VMEM((nchain, 1, ATT_TQ), F32),
            pltpu.VMEM((nchain, HEAD_DIM, ATT_TQ), F32),
            pltpu.VMEM((2, nchain, ATT_TK, ATT_TQ), F32),
            pltpu.VMEM((2, nchain, 1, ATT_TQ), F32),
        ],
        compiler_params=pltpu.CompilerParams(dimension_semantics=("parallel", "parallel", "arbitrary"),
                                             vmem_limit_bytes=VMEM_LIMIT),
        name="attention",
    )(qt, k, vt)


def _memkv_kernel(mem_ref, gm_ref, w_ref, gk_ref, kt_ref, v_ref):
    m = mem_ref[0]
    ms = jnp.mean(m * m, axis=-1, keepdims=True)
    mn = (m * lax.rsqrt(ms + EPS) * gm_ref[...]).astype(BF16)
    kv = _dot(mn, w_ref[...])
    for h in range(MEM_HEADS):
        kh = _rms_heads(kv[:, h * HEAD_DIM:(h + 1) * HEAD_DIM], gk_ref[...])
        kt_ref[0, h] = kh.T.astype(BF16)
    v_ref[0] = kv[:, W_C:].astype(BF16)


def _memkv(mem, g_mem, w_mem_kv, g_k_mem):
    nb, n_mem, _ = mem.shape
    return pl.pallas_call(
        _memkv_kernel,
        grid=(nb,),
        in_specs=[
            pl.BlockSpec((1, n_mem, D_MODEL), lambda b: (b, 0, 0)),
            pl.BlockSpec((1, D_MODEL), lambda b: (0, 0)),
            pl.BlockSpec((D_MODEL, 2 * W_C), lambda b: (0, 0)),
            pl.BlockSpec((1, HEAD_DIM), lambda b: (0, 0)),
        ],
        out_specs=(
            pl.BlockSpec((1, MEM_HEADS, HEAD_DIM, n_mem), lambda b: (b, 0, 0, 0)),
            pl.BlockSpec((1, n_mem, W_C), lambda b: (b, 0, 0)),
        ),
        out_shape=(
            jax.ShapeDtypeStruct((nb, MEM_HEADS, HEAD_DIM, n_mem), BF16),
            jax.ShapeDtypeStruct((nb, n_mem, W_C), BF16),
        ),
        compiler_params=pltpu.CompilerParams(dimension_semantics=("parallel",), vmem_limit_bytes=VMEM_LIMIT),
        name="memkv",
    )(mem, g_mem, w_mem_kv, g_k_mem)


def _final_kernel(xp_ref, xs_ref, of_ref, ob_ref, za_ref, att_ref, zb_ref, qc_ref, zc_ref, gates_ref, kmt_ref,
                  vm_ref, gout_ref, wda_ref, wdb_ref, wdc_ref, wout_ref, yp_ref, ys_ref, oa_scr, oc_scr,
                  *, n_prompt_blocks):
    o_gdn = of_ref[...] + ob_ref[...]
    for h in range(GDN_HEADS):
        hs = slice(h * GDN_DV, (h + 1) * GDN_DV)
        oa_scr[:, hs] = (_rms_heads(o_gdn[:, hs], gout_ref[...]) * za_ref[:, hs].astype(F32)).astype(BF16)
    ya = _dot(oa_scr[...], wda_ref[...])

    o_att = (att_ref[...].astype(F32) * zb_ref[...].astype(F32)).astype(BF16)
    yb = _dot(o_att, wdb_ref[...])

    for h in range(MEM_HEADS):
        hs = slice(h * HEAD_DIM, (h + 1) * HEAD_DIM)
        s = _dot(qc_ref[:, hs], kmt_ref[0, h])
        p = jnp.exp2(s - jnp.max(s, axis=-1, keepdims=True))
        l = jnp.sum(p, axis=-1, keepdims=True)
        o = _dot(p.astype(BF16), vm_ref[0, :, hs]) * (1.0 / l)
        oc_scr[:, hs] = (o * zc_ref[:, hs].astype(F32)).astype(BF16)
    yc = _dot(oc_scr[...], wdc_ref[...])

    mixed = (gates_ref[:, 0:D_MODEL].astype(F32) * ya
             + gates_ref[:, D_MODEL:2 * D_MODEL].astype(F32) * yb
             + gates_ref[:, 2 * D_MODEL:3 * D_MODEL].astype(F32) * yc)
    delta = _dot(mixed.astype(BF16), wout_ref[...])

    @pl.when(pl.program_id(0) < n_prompt_blocks)
    def _():
        yp_ref[...] = xp_ref[...] + delta

    @pl.when(pl.program_id(0) >= n_prompt_blocks)
    def _():
        ys_ref[...] = xs_ref[...] + delta


def _final(xp, xs, o_f, o_b, za, att, zb, qc, zc, gates, kmt, vm, g_out, wda, wdb, wdc, wout, seq):
    tm = TOK_BLOCK
    npb = xp.shape[0] // tm
    ntok = xp.shape[0] + xs.shape[0]
    nblk = seq // tm
    n_mem = vm.shape[1]
    row = lambda i: (i, 0)
    const = lambda i: (0, 0)
    prompt = lambda i: (jnp.minimum(i, npb - 1), 0)
    sample = lambda i: (jnp.maximum(i - npb, 0), 0)
    tokspec = lambda w: pl.BlockSpec((tm, w), row)
    return pl.pallas_call(
        functools.partial(_final_kernel, n_prompt_blocks=npb),
        grid=(ntok // tm,),
        in_specs=[
            pl.BlockSpec((tm, D_MODEL), prompt), pl.BlockSpec((tm, D_MODEL), sample),
            tokspec(W_A), tokspec(W_A), tokspec(W_A), tokspec(W_B), tokspec(W_B),
            tokspec(W_C), tokspec(W_C), tokspec(N_GATE),
            pl.BlockSpec((1, MEM_HEADS, HEAD_DIM, n_mem), lambda i: (i // nblk, 0, 0, 0)),
            pl.BlockSpec((1, n_mem, W_C), lambda i: (i // nblk, 0, 0)),
            pl.BlockSpec((1, GDN_DV), const),
            pl.BlockSpec((W_A, D_MODEL), const),
            pl.BlockSpec((W_B, D_MODEL), const),
            pl.BlockSpec((W_C, D_MODEL), const),
            pl.BlockSpec((D_MODEL, D_MODEL), const),
        ],
        out_specs=(pl.BlockSpec((tm, D_MODEL), prompt), pl.BlockSpec((tm, D_MODEL), sample)),
        out_shape=(jax.ShapeDtypeStruct(xp.shape, F32), jax.ShapeDtypeStruct(xs.shape, F32)),
        scratch_shapes=[pltpu.VMEM((tm, W_A), BF16), pltpu.VMEM((tm, W_C), BF16)],
        compiler_params=pltpu.CompilerParams(dimension_semantics=("arbitrary",), vmem_limit_bytes=VMEM_LIMIT),
        name="final",
    )(xp, xs, o_f, o_b, za, att, zb, qc, zc, gates, kmt, vm, g_out, wda, wdb, wdc, wout)


def _rope_tables(seq):
    half = HEAD_DIM // 2
    n_rows = seq // GRID_W
    freqs = ROPE_THETA ** (-jnp.arange(0, half, 2, dtype=F32) / half)
    ang_r = jnp.arange(n_rows, dtype=jnp.int32).astype(F32)[:, None] * freqs[None, :]
    ang_c = jnp.arange(GRID_W, dtype=jnp.int32).astype(F32)[:, None] * freqs[None, :]
    rows = lambda t: jnp.repeat(t, GRID_W, axis=0)
    cols = lambda t: jnp.tile(t, (n_rows, 1))
    cos_r, sin_r = rows(jnp.cos(ang_r)), rows(jnp.sin(ang_r))
    cos_c, sin_c = cols(jnp.cos(ang_c)), cols(jnp.sin(ang_c))
    cos = jnp.concatenate([cos_r, cos_r, cos_c, cos_c], axis=-1)
    sin = jnp.concatenate([-sin_r, sin_r, -sin_c, sin_c], axis=-1)
    return cos, sin


def _pad_lanes(v):
    return jnp.pad(v.astype(F32), (0, LANES - v.shape[0]))[None, :]


def _layer(xp, xs, mem, seq, g_norm, g_mem, w_in, w_conv, a_log_f, a_log_b, dt_bias_f, dt_bias_b, g_gdn_out,
           g_q_attn, g_k_attn, g_q_mem, g_k_mem, w_mem_kv, w_down_a, w_down_b, w_down_c, w_out):
    w_a = w_in[:, :OFF_QB].astype(BF16)
    w_b = w_in[:, OFF_QB + N_SCAL_A:].astype(BF16)
    w_s = jnp.pad(w_in[:, OFF_QB:OFF_QB + N_SCAL_A], ((0, 0), (0, LANES - N_SCAL_A))).astype(BF16)
    zeros8 = jnp.zeros((2 * GDN_HEADS,), F32)
    alog = _pad_lanes(jnp.concatenate([zeros8, a_log_f, a_log_b]))
    dtb = _pad_lanes(jnp.concatenate([zeros8, dt_bias_f, dt_bias_b]))
    cos, sin = _rope_tables(seq)

    (qkva, za, qt, k_att, vt, zb, qc, zc, gates, scal, sc3, sct3) = _inproj(
        xp, xs, g_norm[None, :], w_a, w_b, w_s, cos, sin, g_q_attn[None, :], g_k_attn[None, :], g_q_mem[None, :],
        alog, dtb, seq)
    nb = qt.shape[0]
    q_a, k_a, v_a = _gdn_prep(qkva, w_conv, seq)
    o_f, o_b = _gdn_scan(q_a, k_a, v_a, scal, sc3, sct3, seq)
    att = _attention(qt, k_att.reshape(nb, seq, ATT_KV_HEADS * HEAD_DIM), vt, seq)
    kmt, vm = _memkv(mem, g_mem[None, :], w_mem_kv.astype(BF16), g_k_mem[None, :])
    return _final(xp, xs, o_f, o_b, za, att.reshape(nb * seq, W_B), zb, qc, zc, gates, kmt, vm,
                  g_gdn_out[None, :], w_down_a.astype(BF16), w_down_b.astype(BF16), w_down_c.astype(BF16),
                  w_out.astype(BF16), seq)


def kernel(x_prompt, x_sample, mem_prompt, mem_sample, g_norm, g_mem, w_in, w_conv, a_log_f, a_log_b, dt_bias_f,
           dt_bias_b, g_gdn_out, g_q_attn, g_k_attn, g_q_mem, g_k_mem, w_mem_kv, w_down_a, w_down_b, w_down_c,
           w_out):
    bp, seq, _ = x_prompt.shape
    bs, seq_s, _ = x_sample.shape
    assert seq == seq_s and seq % TOK_BLOCK == 0 and TOK_BLOCK == ATT_TK
    assert seq % (ATT_UNROLL * ATT_TK) == 0 and seq % ATT_QBLOCK == 0 and ATT_UNROLL % 2 == 0
    yp = x_prompt.reshape(bp * seq, D_MODEL)
    ys = x_sample.reshape(bs * seq, D_MODEL)
    mem = jnp.concatenate([mem_prompt, mem_sample], axis=0)
    for l in range(g_norm.shape[0]):
        yp, ys = _layer(yp, ys, mem, seq, g_norm[l], g_mem[l], w_in[l], w_conv[l], a_log_f[l], a_log_b[l],
                        dt_bias_f[l], dt_bias_b[l], g_gdn_out[l], g_q_attn[l], g_k_attn[l], g_q_mem[l],
                        g_k_mem[l], w_mem_kv[l], w_down_a[l], w_down_b[l], w_down_c[l], w_out[l])
    return (yp.reshape(bp, seq, D_MODEL), ys.reshape(bs, seq, D_MODEL))
```

```python
import functools
import math

import jax
import jax.numpy as jnp
from jax import lax
from jax.experimental import pallas as pl
from jax.experimental.pallas import tpu as pltpu

F32 = jnp.float32
BF16 = jnp.bfloat16

D_MODEL = 1024
GRID_W = 64
HEAD_DIM = 128
EPS = 1e-6
GDN_HEADS = 4
GDN_DK = 128
GDN_DV = 128
GDN_CONV = 5
GDN_CHUNK = 64
ATT_HEADS = 8
ATT_KV_HEADS = 2
ROPE_THETA = 10000.0
MEM_HEADS = 4
N_BRANCH = 3

W_A = GDN_HEADS * GDN_DV
W_B = ATT_HEADS * HEAD_DIM
W_C = MEM_HEADS * HEAD_DIM
N_QKV_A = 2 * GDN_HEADS * GDN_DK + GDN_HEADS * GDN_DV
N_SCAL_A = 4 * GDN_HEADS
N_KV_B = 2 * ATT_KV_HEADS * HEAD_DIM
N_GATE = N_BRANCH * D_MODEL

LANES = 128
LOG2E = math.log2(math.e)
SM_SCALE = HEAD_DIM ** -0.5 * LOG2E

OFF_QKVA = 0
OFF_ZA = OFF_QKVA + N_QKV_A
OFF_QB = OFF_ZA + W_A
OFF_KVB = OFF_QB + W_B
OFF_ZB = OFF_KVB + N_KV_B
OFF_QC = OFF_ZB + W_B
OFF_ZC = OFF_QC + W_C
OFF_GATE = OFF_ZC + W_C
OFF_SCAL = OFF_GATE + N_GATE

TOK_BLOCK = 512
GDN_BLOCK = 256
GDN_SUB = 2
ATT_TQ = 256
ATT_QBLOCK = 512
ATT_UNROLL = 8
ATT_TK = 512
N_SPLIT = 3
VMEM_LIMIT = 56 * 1024 * 1024


def _sigmoid(x):
    return 1.0 / (1.0 + jnp.exp(-x))


def _softplus(x):
    return jnp.maximum(x, 0.0) + jnp.log(1.0 + jnp.exp(-jnp.abs(x)))


def _rms_heads(y, g):
    ms = jnp.mean(y * y, axis=-1, keepdims=True)
    return y * lax.rsqrt(ms + EPS) * g


def _rope(y, cos, sin_signed):
    lane = lax.broadcasted_iota(jnp.int32, y.shape, 1)
    fwd = pltpu.roll(y, 32, axis=1)
    bwd = pltpu.roll(y, 96, axis=1)
    partner = jnp.where((lane & 32) == 0, bwd, fwd)
    return y * cos + partner * sin_signed


def _dot(a, b):
    return jnp.dot(a, b, preferred_element_type=F32)


def _inproj_kernel(xp_ref, xs_ref, gn_ref, wa_ref, wb_ref, ws_ref, cos_ref, sin_ref, gq_ref, gk_ref, gqm_ref,
                   alog_ref, dtb_ref, qkva_ref, za_ref, qt_ref, k_ref, vt_ref, zb_ref, qc_ref, zc_ref, gates_ref,
                   scal_ref, sc3_ref, sct3_ref, *, n_prompt_blocks):
    x = jnp.where(pl.program_id(0) < n_prompt_blocks, xp_ref[...], xs_ref[...])
    ms = jnp.mean(x * x, axis=-1, keepdims=True)
    h = (x * lax.rsqrt(ms + EPS) * gn_ref[...]).astype(BF16)

    def mm(off, n):
        if off == OFF_SCAL:
            w = ws_ref[...]
        elif off < OFF_QB:
            w = wa_ref[:, off:off + n]
        else:
            w = wb_ref[:, off - OFF_QB:off - OFF_QB + n]
        return _dot(h, w)

    step = 512
    for c in range(0, N_QKV_A, step):
        qkva_ref[:, c:c + step] = mm(OFF_QKVA + c, step).astype(BF16)

    z = mm(OFF_ZA, W_A)
    za_ref[...] = (z * _sigmoid(z)).astype(BF16)

    cos = cos_ref[...]
    sin = sin_ref[...]
    for c in range(0, W_B, step):
        y = mm(OFF_QB + c, step)
        for j in range(step // HEAD_DIM):
            yh = _rope(_rms_heads(y[:, j * HEAD_DIM:(j + 1) * HEAD_DIM], gq_ref[...]), cos, sin)
            qt_ref[0, c // HEAD_DIM + j] = (yh * SM_SCALE).T.astype(BF16)

    y = mm(OFF_KVB, N_KV_B)
    for g in range(ATT_KV_HEADS):
        kh = _rope(_rms_heads(y[:, g * HEAD_DIM:(g + 1) * HEAD_DIM], gk_ref[...]), cos, sin)
        k_ref[:, g * HEAD_DIM:(g + 1) * HEAD_DIM] = kh.astype(BF16)
        vh = y[:, (ATT_KV_HEADS + g) * HEAD_DIM:(ATT_KV_HEADS + g + 1) * HEAD_DIM]
        vt_ref[0, g, 0] = vh.T.astype(BF16)

    for c in range(0, W_B, step):
        z = mm(OFF_ZB + c, step)
        zb_ref[:, c:c + step] = (z * _sigmoid(z)).astype(BF16)

    y = mm(OFF_QC, W_C)
    for j in range(MEM_HEADS):
        yh = _rms_heads(y[:, j * HEAD_DIM:(j + 1) * HEAD_DIM], gqm_ref[...])
        qc_ref[:, j * HEAD_DIM:(j + 1) * HEAD_DIM] = (yh * SM_SCALE).astype(BF16)

    z = mm(OFF_ZC, W_C)
    zc_ref[...] = (z * _sigmoid(z)).astype(BF16)

    for c in range(0, N_GATE, step):
        gates_ref[:, c:c + step] = _sigmoid(mm(OFF_GATE + c, step)).astype(BF16)

    s = mm(OFF_SCAL, LANES)
    lane = lax.broadcasted_iota(jnp.int32, s.shape, 1)
    act = jnp.where(lane < 2 * GDN_HEADS, _sigmoid(s), -jnp.exp(alog_ref[...]) * _softplus(s + dtb_ref[...]))
    act = jnp.where(lane < N_SCAL_A, act, 0.0)
    scal_ref[...] = act[:, :N_SCAL_A]
    hi = act.astype(BF16).astype(F32)
    rem = act - hi
    mid = rem.astype(BF16).astype(F32)
    low = rem - mid
    packed = hi + pltpu.roll(mid, N_SCAL_A, axis=1) + pltpu.roll(low, 2 * N_SCAL_A, axis=1)
    sc3_ref[...] = packed.astype(BF16)
    sct3_ref[...] = packed.T[:N_SPLIT * N_SCAL_A, :].astype(BF16)


def _inproj(xp, xs, g_norm, w_a, w_b, w_s, cos, sin, g_q, g_k, g_qm, alog, dtb, seq):
    tm = TOK_BLOCK
    npb = xp.shape[0] // tm
    ntok = xp.shape[0] + xs.shape[0]
    nblk = seq // tm
    nb = ntok // seq
    row = lambda i: (i, 0)
    const = lambda i: (0, 0)
    pos = lambda i: (i % nblk, 0)
    vec = pl.BlockSpec((1, LANES), const)
    out_shape = (
        jax.ShapeDtypeStruct((ntok, N_QKV_A), BF16),
        jax.ShapeDtypeStruct((ntok, W_A), BF16),
        jax.ShapeDtypeStruct((nb, ATT_HEADS, HEAD_DIM, seq), BF16),
        jax.ShapeDtypeStruct((ntok, ATT_KV_HEADS * HEAD_DIM), BF16),
        jax.ShapeDtypeStruct((nb, ATT_KV_HEADS, seq // ATT_TK, HEAD_DIM, ATT_TK), BF16),
        jax.ShapeDtypeStruct((ntok, W_B), BF16),
        jax.ShapeDtypeStruct((ntok, W_C), BF16),
        jax.ShapeDtypeStruct((ntok, W_C), BF16),
        jax.ShapeDtypeStruct((ntok, N_GATE), BF16),
        jax.ShapeDtypeStruct((ntok, N_SCAL_A), F32),
        jax.ShapeDtypeStruct((ntok, LANES), BF16),
        jax.ShapeDtypeStruct((N_SPLIT * N_SCAL_A, ntok), BF16),
    )
    out_specs = (
        pl.BlockSpec((tm, N_QKV_A), row),
        pl.BlockSpec((tm, W_A), row),
        pl.BlockSpec((1, ATT_HEADS, HEAD_DIM, tm), lambda i: (i // nblk, 0, 0, i % nblk)),
        pl.BlockSpec((tm, ATT_KV_HEADS * HEAD_DIM), row),
        pl.BlockSpec((1, ATT_KV_HEADS, 1, HEAD_DIM, ATT_TK), lambda i: (i // nblk, 0, i % nblk, 0, 0)),
        pl.BlockSpec((tm, W_B), row),
        pl.BlockSpec((tm, W_C), row),
        pl.BlockSpec((tm, W_C), row),
        pl.BlockSpec((tm, N_GATE), row),
        pl.BlockSpec((tm, N_SCAL_A), row),
        pl.BlockSpec((tm, LANES), row),
        pl.BlockSpec((N_SPLIT * N_SCAL_A, tm), lambda i: (0, i)),
    )
    in_specs = [
        pl.BlockSpec((tm, D_MODEL), lambda i: (jnp.minimum(i, npb - 1), 0)),
        pl.BlockSpec((tm, D_MODEL), lambda i: (jnp.maximum(i - npb, 0), 0)),
        pl.BlockSpec((1, D_MODEL), const),
        pl.BlockSpec((D_MODEL, OFF_QB), const, pipeline_mode=pl.Buffered(1)),
        pl.BlockSpec((D_MODEL, OFF_SCAL - OFF_QB), const, pipeline_mode=pl.Buffered(1)),
        pl.BlockSpec((D_MODEL, LANES), const, pipeline_mode=pl.Buffered(1)),
        pl.BlockSpec((tm, LANES), pos),
        pl.BlockSpec((tm, LANES), pos),
        vec, vec, vec, vec, vec,
    ]
    return pl.pallas_call(
        functools.partial(_inproj_kernel, n_prompt_blocks=npb),
        grid=(ntok // tm,),
        in_specs=in_specs,
        out_specs=out_specs,
        out_shape=out_shape,
        compiler_params=pltpu.CompilerParams(dimension_semantics=("parallel",), vmem_limit_bytes=VMEM_LIMIT),
        name="inproj",
    )(xp, xs, g_norm, w_a, w_b, w_s, cos, sin, g_q, g_k, g_qm, alog, dtb)


HALO = 8


def _gdn_prep_kernel(prev_ref, cur_ref, next_ref, wc_ref, q_ref, k_ref, v_ref, xs_scr, *, nblk):
    tm = cur_ref.shape[0]
    sblk = pl.program_id(0) % nblk
    xs_scr[HALO:HALO + tm, :] = cur_ref[...].astype(F32)
    xs_scr[0:HALO, :] = jnp.where(sblk == 0, 0.0, prev_ref[...].astype(F32))
    xs_scr[HALO + tm:2 * HALO + tm, :] = jnp.where(sblk == nblk - 1, 0.0, next_ref[...].astype(F32))
    pad = (GDN_CONV - 1) // 2
    nqk = GDN_HEADS * GDN_DK
    for c in range(0, N_QKV_A, LANES):
        acc = None
        for j in range(GDN_CONV):
            t = xs_scr[HALO - pad + j:HALO - pad + j + tm, c:c + LANES] * wc_ref[j:j + 1, c:c + LANES]
            acc = t if acc is None else acc + t
        y = acc * _sigmoid(acc)
        if c < 2 * nqk:
            y = y * lax.rsqrt(jnp.sum(y * y, axis=-1, keepdims=True) + EPS)
        if c < nqk:
            q_ref[:, c:c + LANES] = (y * (GDN_DK ** -0.5)).astype(BF16)
        elif c < 2 * nqk:
            k_ref[:, c - nqk:c - nqk + LANES] = y.astype(BF16)
        else:
            v_ref[:, c - 2 * nqk:c - 2 * nqk + LANES] = y.astype(BF16)


def _gdn_prep(qkva, w_conv, seq):
    ntok = qkva.shape[0]
    tm = TOK_BLOCK
    nblk = seq // tm
    hb = tm // HALO
    nhalo = ntok // HALO
    out = jax.ShapeDtypeStruct((ntok, W_A), BF16)
    row = lambda i: (i, 0)
    return pl.pallas_call(
        functools.partial(_gdn_prep_kernel, nblk=nblk),
        grid=(ntok // tm,),
        in_specs=[
            pl.BlockSpec((HALO, N_QKV_A), lambda i: (jnp.maximum(i * hb - 1, 0), 0)),
            pl.BlockSpec((tm, N_QKV_A), row),
            pl.BlockSpec((HALO, N_QKV_A), lambda i: (jnp.minimum((i + 1) * hb, nhalo - 1), 0)),
            pl.BlockSpec((GDN_CONV, N_QKV_A), lambda i: (0, 0)),
        ],
        out_specs=(pl.BlockSpec((tm, W_A), row),) * 3,
        out_shape=(out, out, out),
        scratch_shapes=[pltpu.VMEM((tm + 2 * HALO, N_QKV_A), F32)],
        compiler_params=pltpu.CompilerParams(dimension_semantics=("parallel",), vmem_limit_bytes=VMEM_LIMIT),
        name="gdn_prep",
    )(qkva, qkva, qkva, w_conv)


def _gdn_scan_kernel(qf_ref, kf_ref, vf_ref, scf_ref, sc3f_ref, sct3f_ref,
                     qb_ref, kb_ref, vb_ref, scb_ref, sc3b_ref, sct3b_ref, of_ref, ob_ref, state_ref):
    @pl.when(pl.program_id(1) == 0)
    def _():
        state_ref[...] = jnp.zeros_like(state_ref)

    n = GDN_BLOCK
    nchunk = n // GDN_CHUNK
    shift = GDN_CHUNK.bit_length() - 1
    row = lax.broadcasted_iota(jnp.int32, (n, n), 0)
    col = lax.broadcasted_iota(jnp.int32, (n, n), 1)
    same = (row >> shift) == (col >> shift)
    one = jnp.float32(1.0)
    zero = jnp.float32(0.0)
    rchunk = lax.broadcasted_iota(jnp.int32, (n, 2 * GDN_DV), 0) >> shift
    nsc = N_SCAL_A
    eye_w = jnp.where(lax.broadcasted_iota(jnp.int32, (GDN_CHUNK, n), 0)
                      == (lax.broadcasted_iota(jnp.int32, (GDN_CHUNK, n), 1) & (GDN_CHUNK - 1)), one, zero)

    def wide(x):
        out = x[0:GDN_CHUNK]
        for c in range(1, nchunk):
            out = out + x[c * GDN_CHUNK:(c + 1) * GDN_CHUNK]
        return out

    def block_diag(w):
        return jnp.where(same, jnp.concatenate([w] * nchunk, axis=0), 0.0).astype(BF16)

    probs = []
    for d, (q_ref, k_ref, v_ref, sc_ref, sc3_ref, sct3_ref, o_ref) in enumerate((
            (qf_ref, kf_ref, vf_ref, scf_ref, sc3f_ref, sct3f_ref, of_ref),
            (qb_ref, kb_ref, vb_ref, scb_ref, sc3b_ref, sct3b_ref, ob_ref))):
        if d == 1:
            incl = same & (row <= col)
            strict = same & (row < col)
            incl_t = same & (row >= col)
        else:
            incl = same & (row >= col)
            strict = same & (row > col)
            incl_t = same & (row <= col)
        m01 = jnp.concatenate([jnp.where(incl, one, zero), jnp.where(same, one, zero)], axis=0).astype(BF16)
        m01_t = jnp.where(incl_t, one, zero).astype(BF16)
        for sb in range(GDN_SUB):
            rs = slice(sb * n, (sb + 1) * n)
            gg = _dot(m01, sc3_ref[rs, :])
            gg = gg + pltpu.roll(gg, LANES - nsc, axis=1) + pltpu.roll(gg, LANES - 2 * nsc, axis=1)
            gr3 = _dot(sct3_ref[:, rs], m01_t)
            g_row_all = gr3[0:nsc] + gr3[nsc:2 * nsc] + gr3[2 * nsc:3 * nsc]
            sc = sc_ref[rs, :]
            for h in range(GDN_HEADS):
                cb = d * GDN_HEADS + h
                cg = 2 * GDN_HEADS + cb
                hs = slice(h * GDN_DK, (h + 1) * GDN_DK)
                probs.append(dict(
                    idx=cb, sb=sb, rs=rs, incl=incl, strict=strict, o_ref=o_ref,
                    beta=sc[:, cb:cb + 1], gc=gg[:n, cg:cg + 1], gl=gg[n:, cg:cg + 1], gr=g_row_all[cg:cg + 1, :],
                    q_ref=q_ref, k_ref=k_ref, v_ref=v_ref, hs=hs))

    for p in probs:
        incl = p["incl"]
        decay = jnp.where(incl, jnp.exp(jnp.where(incl, p["gc"] - p["gr"], 0.0)), 0.0)
        k_bf = p["k_ref"][p["rs"], p["hs"]]
        q_bf = p["q_ref"][p["rs"], p["hs"]]
        kb = k_bf.astype(F32) * p["beta"]
        lhs = jnp.concatenate([kb.astype(BF16), q_bf], axis=0)
        kq = lax.dot_general(lhs, k_bf, (((1,), (1,)), ((), ())), preferred_element_type=F32)
        a_neg = jnp.where(p["strict"], -(kq[:n] * decay), 0.0)
        p["qk"] = (kq[n:] * decay).astype(BF16)
        p["pd"] = a_neg.astype(BF16)
        p["pw"] = wide(a_neg)
        p["kb"] = kb

    for p in probs:
        p["tw"] = eye_w + p["pw"]
        p["pw"] = _dot(p["pw"].astype(BF16), p["pd"])
    for it in range(1, shift):
        for p in probs:
            p["pd"] = block_diag(p["pw"])
        last = it == shift - 1
        for p in probs:
            if last:
                p["tw"] = p["tw"] + _dot(p["tw"].astype(BF16), p["pd"])
            else:
                both = _dot(jnp.concatenate([p["tw"], p["pw"]], axis=0).astype(BF16), p["pd"])
                p["tw"] = p["tw"] + both[:GDN_CHUNK]
                p["pw"] = both[GDN_CHUNK:]
    for p in probs:
        p["t"] = block_diag(p["tw"])

    for p in probs:
        e_g = jnp.exp(p["gc"])
        v = p["v_ref"][p["rs"], p["hs"]].astype(F32)
        rhs = jnp.concatenate([(v * p["beta"]).astype(BF16), (p["kb"] * e_g).astype(BF16)], axis=1)
        p["uw"] = _dot(p["t"], rhs)
        p["e_g"] = e_g

    for p in probs:
        qkuw = _dot(p["qk"], p["uw"].astype(BF16))
        p["local"] = qkuw[:, :GDN_DV]
        q = p["q_ref"][p["rs"], p["hs"]].astype(F32)
        p["q_eff"] = (q * p["e_g"] - qkuw[:, GDN_DV:]).astype(BF16)
        k = p["k_ref"][p["rs"], p["hs"]].astype(F32)
        p["kdec_t"] = (k * jnp.exp(p["gl"] - p["gc"])).T.astype(BF16)
        p["e_gl"] = jnp.exp(p["gl"])

    for p in probs:
        p["kx"] = [_dot(p["kdec_t"], jnp.where(rchunk == c, p["uw"], 0.0).astype(BF16)) for c in range(nchunk)]

    by_key = {(p["idx"], p["sb"]): p for p in probs}
    state = [state_ref[i] for i in range(2 * GDN_HEADS)]
    for sbi in range(GDN_SUB):
        for step in range(nchunk):
            for i in range(2 * GDN_HEADS):
                reverse = i >= GDN_HEADS
                p = by_key[(i, GDN_SUB - 1 - sbi if reverse else sbi)]
                c = nchunk - 1 - step if reverse else step
                r0 = c * GDN_CHUNK
                s_bf = state[i].astype(BF16)
                o_c = _dot(p["q_eff"][r0:r0 + GDN_CHUNK], s_bf) + p["local"][r0:r0 + GDN_CHUNK]
                o0 = p["sb"] * n + r0
                p["o_ref"][o0:o0 + GDN_CHUNK, p["hs"]] = o_c
                kx = p["kx"][c]
                state[i] = (p["e_gl"][r0:r0 + 1, :] * state[i] - _dot(kx[:, GDN_DV:].astype(BF16), s_bf)
                            + kx[:, :GDN_DV])
    for i in range(2 * GDN_HEADS):
        state_ref[i] = state[i]


def _gdn_scan(q, k, v, scal, sc3, sct3, seq):
    ntok = q.shape[0]
    n = GDN_SUB * GDN_BLOCK
    nt = seq // n
    nb = ntok // seq
    fwd = lambda b, j: (b * nt + j, 0)
    bwd = lambda b, j: (b * nt + nt - 1 - j, 0)
    fwd_t = lambda b, j: (0, b * nt + j)
    bwd_t = lambda b, j: (0, b * nt + nt - 1 - j)
    tok = lambda m: pl.BlockSpec((n, W_A), m)
    side = lambda m, mt: [tok(m), tok(m), tok(m), pl.BlockSpec((n, N_SCAL_A), m), pl.BlockSpec((n, LANES), m),
                          pl.BlockSpec((N_SPLIT * N_SCAL_A, n), mt)]
    out = jax.ShapeDtypeStruct((ntok, W_A), F32)
    return pl.pallas_call(
        _gdn_scan_kernel,
        grid=(nb, nt),
        in_specs=side(fwd, fwd_t) + side(bwd, bwd_t),
        out_specs=(tok(fwd), tok(bwd)),
        out_shape=(out, out),
        scratch_shapes=[pltpu.VMEM((2 * GDN_HEADS, GDN_DK, GDN_DV), F32)],
        compiler_params=pltpu.CompilerParams(dimension_semantics=("parallel", "arbitrary"),
                                             vmem_limit_bytes=VMEM_LIMIT),
        name="gdn_scan",
    )(q, k, v, scal, sc3, sct3, q, k, v, scal, sc3, sct3)


def _attn_kernel(qt_ref, qtn_ref, k_ref, vt_ref, o_ref, m_scr, l_scr, acc_scr, s_scr, mb_scr, q2_scr):
    grp = qt_ref.shape[1]
    tq = qt_ref.shape[3]
    nkv = vt_ref.shape[2]
    ntile = tq // ATT_TQ
    chains = [(h, t) for h in range(grp) for t in range(ntile)]
    m_scr[...] = jnp.full_like(m_scr, -jnp.inf)
    l_scr[...] = jnp.zeros_like(l_scr)
    acc_scr[...] = jnp.zeros_like(acc_scr)

    q2_scr[0] = qt_ref[0]
    q2_scr[1] = qtn_ref[0]

    def scores(j, slot, c, which):
        h, t = chains[c]
        start = pl.multiple_of(j * ATT_TK, ATT_TK)
        s = _dot(k_ref[0, pl.ds(start, ATT_TK), :], q2_scr[which, h, :, t * ATT_TQ:(t + 1) * ATT_TQ])
        s_scr[slot, c] = s
        mb_scr[slot, c] = jnp.max(s, axis=0, keepdims=True)

    def update(j, slot, c):
        m_prev = m_scr[c]
        m_new = jnp.maximum(m_prev, mb_scr[slot, c])
        alpha = jnp.exp2(m_prev - m_new)
        p = jnp.exp2(s_scr[slot, c] - m_new)
        l_scr[c] = alpha * l_scr[c] + jnp.sum(p, axis=0, keepdims=True)
        acc_scr[c] = alpha * acc_scr[c] + _dot(vt_ref[0, 0, j], p.astype(BF16))
        m_scr[c] = m_new

    def stage(j, slot):
        wrap = (j + 1) // nkv
        j_next = j + 1 - wrap * nkv
        for c in range(len(chains)):
            scores(j_next, 1 - slot, c, wrap)
            update(j, slot, c)

    @pl.when(pl.program_id(2) == 0)
    def _():
        for c in range(len(chains)):
            scores(0, 0, c, 0)

    def body(i, carry):
        for u in range(ATT_UNROLL):
            stage(ATT_UNROLL * i + u, u % 2)
        return carry

    lax.fori_loop(0, nkv // ATT_UNROLL, body, 0)
    for c, (h, t) in enumerate(chains):
        out = acc_scr[c] * (1.0 / l_scr[c])
        o_ref[0, t * ATT_TQ:(t + 1) * ATT_TQ, h * HEAD_DIM:(h + 1) * HEAD_DIM] = out.T.astype(o_ref.dtype)


def _attention(qt, k, vt, seq):
    nb = qt.shape[0]
    grp = ATT_HEADS // ATT_KV_HEADS
    tq = ATT_QBLOCK
    nchain = grp * (tq // ATT_TQ)
    nq = seq // tq
    return pl.pallas_call(
        _attn_kernel,
        grid=(nb, ATT_KV_HEADS, nq),
        in_specs=[
            pl.BlockSpec((1, grp, HEAD_DIM, tq), lambda b, g, i: (b, g, 0, i)),
            pl.BlockSpec((1, grp, HEAD_DIM, tq), lambda b, g, i: (b, g, 0, jnp.minimum(i + 1, nq - 1))),
            pl.BlockSpec((1, seq, HEAD_DIM), lambda b, g, i: (b, 0, g)),
            pl.BlockSpec((1, 1, seq // ATT_TK, HEAD_DIM, ATT_TK), lambda b, g, i: (b, g, 0, 0, 0)),
        ],
        out_specs=pl.BlockSpec((1, tq, grp * HEAD_DIM), lambda b, g, i: (b, i, g)),
        out_shape=jax.ShapeDtypeStruct((nb, seq, W_B), BF16),
        scratch_shapes=[
            pltpu.VMEM((nchain, 1, ATT_TQ), F32),
            pltpu.VMEM((nchain, 1, ATT_TQ), F32),
            pltpu.VMEM((nchain, HEAD_DIM, ATT_TQ), F32),
            pltpu.VMEM((2, nchain, ATT_TK, ATT_TQ), F32),
            pltpu.VMEM((2, nchain, 1, ATT_TQ), F32),
            pltpu.VMEM((2, grp, HEAD_DIM, tq), BF16),
        ],
        compiler_params=pltpu.CompilerParams(dimension_semantics=("parallel", "parallel", "arbitrary"),
                                             vmem_limit_bytes=VMEM_LIMIT),
        name="attention",
    )(qt, qt, k, vt)


def _memkv_kernel(mem_ref, gm_ref, w_ref, gk_ref, kt_ref, v_ref):
    m = mem_ref[0]
    ms = jnp.mean(m * m, axis=-1, keepdims=True)
    mn = (m * lax.rsqrt(ms + EPS) * gm_ref[...]).astype(BF16)
    kv = _dot(mn, w_ref[...])
    for h in range(MEM_HEADS):
        kh = _rms_heads(kv[:, h * HEAD_DIM:(h + 1) * HEAD_DIM], gk_ref[...])
        kt_ref[0, h] = kh.T.astype(BF16)
    v_ref[0] = kv[:, W_C:].astype(BF16)


def _memkv(mem, g_mem, w_mem_kv, g_k_mem):
    nb, n_mem, _ = mem.shape
    return pl.pallas_call(
        _memkv_kernel,
        grid=(nb,),
        in_specs=[
            pl.BlockSpec((1, n_mem, D_MODEL), lambda b: (b, 0, 0)),
            pl.BlockSpec((1, D_MODEL), lambda b: (0, 0)),
            pl.BlockSpec((D_MODEL, 2 * W_C), lambda b: (0, 0)),
            pl.BlockSpec((1, HEAD_DIM), lambda b: (0, 0)),
        ],
        out_specs=(
            pl.BlockSpec((1, MEM_HEADS, HEAD_DIM, n_mem), lambda b: (b, 0, 0, 0)),
            pl.BlockSpec((1, n_mem, W_C), lambda b: (b, 0, 0)),
        ),
        out_shape=(
            jax.ShapeDtypeStruct((nb, MEM_HEADS, HEAD_DIM, n_mem), BF16),
            jax.ShapeDtypeStruct((nb, n_mem, W_C), BF16),
        ),
        compiler_params=pltpu.CompilerParams(dimension_semantics=("parallel",), vmem_limit_bytes=VMEM_LIMIT),
        name="memkv",
    )(mem, g_mem, w_mem_kv, g_k_mem)


def _final_kernel(xp_ref, xs_ref, of_ref, ob_ref, za_ref, att_ref, zb_ref, qc_ref, zc_ref, gates_ref, kmt_ref,
                  vm_ref, gout_ref, wda_ref, wdb_ref, wdc_ref, wout_ref, yp_ref, ys_ref, oa_scr, oc_scr,
                  *, n_prompt_blocks):
    o_gdn = of_ref[...] + ob_ref[...]
    for h in range(GDN_HEADS):
        hs = slice(h * GDN_DV, (h + 1) * GDN_DV)
        oa_scr[:, hs] = (_rms_heads(o_gdn[:, hs], gout_ref[...]) * za_ref[:, hs].astype(F32)).astype(BF16)
    ya = _dot(oa_scr[...], wda_ref[...])

    o_att = (att_ref[...].astype(F32) * zb_ref[...].astype(F32)).astype(BF16)
    yb = _dot(o_att, wdb_ref[...])

    for h in range(MEM_HEADS):
        hs = slice(h * HEAD_DIM, (h + 1) * HEAD_DIM)
        s = _dot(qc_ref[:, hs], kmt_ref[0, h])
        p = jnp.exp2(s - jnp.max(s, axis=-1, keepdims=True))
        l = jnp.sum(p, axis=-1, keepdims=True)
        o = _dot(p.astype(BF16), vm_ref[0, :, hs]) * (1.0 / l)
        oc_scr[:, hs] = (o * zc_ref[:, hs].astype(F32)).astype(BF16)
    yc = _dot(oc_scr[...], wdc_ref[...])

    mixed = (gates_ref[:, 0:D_MODEL].astype(F32) * ya
             + gates_ref[:, D_MODEL:2 * D_MODEL].astype(F32) * yb
             + gates_ref[:, 2 * D_MODEL:3 * D_MODEL].astype(F32) * yc)
    delta = _dot(mixed.astype(BF16), wout_ref[...])

    @pl.when(pl.program_id(0) < n_prompt_blocks)
    def _():
        yp_ref[...] = xp_ref[...] + delta

    @pl.when(pl.program_id(0) >= n_prompt_blocks)
    def _():
        ys_ref[...] = xs_ref[...] + delta


def _final(xp, xs, o_f, o_b, za, att, zb, qc, zc, gates, kmt, vm, g_out, wda, wdb, wdc, wout, seq):
    tm = TOK_BLOCK
    npb = xp.shape[0] // tm
    ntok = xp.shape[0] + xs.shape[0]
    nblk = seq // tm
    n_mem = vm.shape[1]
    row = lambda i: (i, 0)
    const = lambda i: (0, 0)
    prompt = lambda i: (jnp.minimum(i, npb - 1), 0)
    sample = lambda i: (jnp.maximum(i - npb, 0), 0)
    tokspec = lambda w: pl.BlockSpec((tm, w), row)
    return pl.pallas_call(
        functools.partial(_final_kernel, n_prompt_blocks=npb),
        grid=(ntok // tm,),
        in_specs=[
            pl.BlockSpec((tm, D_MODEL), prompt), pl.BlockSpec((tm, D_MODEL), sample),
            tokspec(W_A), tokspec(W_A), tokspec(W_A), tokspec(W_B), tokspec(W_B),
            tokspec(W_C), tokspec(W_C), tokspec(N_GATE),
            pl.BlockSpec((1, MEM_HEADS, HEAD_DIM, n_mem), lambda i: (i // nblk, 0, 0, 0)),
            pl.BlockSpec((1, n_mem, W_C), lambda i: (i // nblk, 0, 0)),
            pl.BlockSpec((1, GDN_DV), const),
            pl.BlockSpec((W_A, D_MODEL), const),
            pl.BlockSpec((W_B, D_MODEL), const),
            pl.BlockSpec((W_C, D_MODEL), const),
            pl.BlockSpec((D_MODEL, D_MODEL), const),
        ],
        out_specs=(pl.BlockSpec((tm, D_MODEL), prompt), pl.BlockSpec((tm, D_MODEL), sample)),
        out_shape=(jax.ShapeDtypeStruct(xp.shape, F32), jax.ShapeDtypeStruct(xs.shape, F32)),
        scratch_shapes=[pltpu.VMEM((tm, W_A), BF16), pltpu.VMEM((tm, W_C), BF16)],
        compiler_params=pltpu.CompilerParams(dimension_semantics=("arbitrary",), vmem_limit_bytes=VMEM_LIMIT),
        name="final",
    )(xp, xs, o_f, o_b, za, att, zb, qc, zc, gates, kmt, vm, g_out, wda, wdb, wdc, wout)


def _rope_tables(seq):
    half = HEAD_DIM // 2
    n_rows = seq // GRID_W
    freqs = ROPE_THETA ** (-jnp.arange(0, half, 2, dtype=F32) / half)
    ang_r = jnp.arange(n_rows, dtype=jnp.int32).astype(F32)[:, None] * freqs[None, :]
    ang_c = jnp.arange(GRID_W, dtype=jnp.int32).astype(F32)[:, None] * freqs[None, :]
    rows = lambda t: jnp.repeat(t, GRID_W, axis=0)
    cols = lambda t: jnp.tile(t, (n_rows, 1))
    cos_r, sin_r = rows(jnp.cos(ang_r)), rows(jnp.sin(ang_r))
    cos_c, sin_c = cols(jnp.cos(ang_c)), cols(jnp.sin(ang_c))
    cos = jnp.concatenate([cos_r, cos_r, cos_c, cos_c], axis=-1)
    sin = jnp.concatenate([-sin_r, sin_r, -sin_c, sin_c], axis=-1)
    return cos, sin


def _pad_lanes(v):
    return jnp.pad(v.astype(F32), (0, LANES - v.shape[0]))[None, :]


def _layer(xp, xs, mem, seq, g_norm, g_mem, w_in, w_conv, a_log_f, a_log_b, dt_bias_f, dt_bias_b, g_gdn_out,
           g_q_attn, g_k_attn, g_q_mem, g_k_mem, w_mem_kv, w_down_a, w_down_b, w_down_c, w_out):
    w_a = w_in[:, :OFF_QB].astype(BF16)
    w_b = w_in[:, OFF_QB + N_SCAL_A:].astype(BF16)
    w_s = jnp.pad(w_in[:, OFF_QB:OFF_QB + N_SCAL_A], ((0, 0), (0, LANES - N_SCAL_A))).astype(BF16)
    zeros8 = jnp.zeros((2 * GDN_HEADS,), F32)
    alog = _pad_lanes(jnp.concatenate([zeros8, a_log_f, a_log_b]))
    dtb = _pad_lanes(jnp.concatenate([zeros8, dt_bias_f, dt_bias_b]))
    cos, sin = _rope_tables(seq)

    (qkva, za, qt, k_att, vt, zb, qc, zc, gates, scal, sc3, sct3) = _inproj(
        xp, xs, g_norm[None, :], w_a, w_b, w_s, cos, sin, g_q_attn[None, :], g_k_attn[None, :], g_q_mem[None, :],
        alog, dtb, seq)
    nb = qt.shape[0]
    q_a, k_a, v_a = _gdn_prep(qkva, w_conv, seq)
    o_f, o_b = _gdn_scan(q_a, k_a, v_a, scal, sc3, sct3, seq)
    att = _attention(qt, k_att.reshape(nb, seq, ATT_KV_HEADS * HEAD_DIM), vt, seq)
    kmt, vm = _memkv(mem, g_mem[None, :], w_mem_kv.astype(BF16), g_k_mem[None, :])
    return _final(xp, xs, o_f, o_b, za, att.reshape(nb * seq, W_B), zb, qc, zc, gates, kmt, vm,
                  g_gdn_out[None, :], w_down_a.astype(BF16), w_down_b.astype(BF16), w_down_c.astype(BF16),
                  w_out.astype(BF16), seq)


def kernel(x_prompt, x_sample, mem_prompt, mem_sample, g_norm, g_mem, w_in, w_conv, a_log_f, a_log_b, dt_bias_f,
           dt_bias_b, g_gdn_out, g_q_attn, g_k_attn, g_q_mem, g_k_mem, w_mem_kv, w_down_a, w_down_b, w_down_c,
           w_out):
    bp, seq, _ = x_prompt.shape
    bs, seq_s, _ = x_sample.shape
    assert seq == seq_s and seq % TOK_BLOCK == 0 and TOK_BLOCK == ATT_TK
    assert seq % (ATT_UNROLL * ATT_TK) == 0 and seq % ATT_QBLOCK == 0 and ATT_UNROLL % 2 == 0
    assert seq % (GDN_SUB * GDN_BLOCK) == 0
    yp = x_prompt.reshape(bp * seq, D_MODEL)
    ys = x_sample.reshape(bs * seq, D_MODEL)
    mem = jnp.concatenate([mem_prompt, mem_sample], axis=0)
    for l in range(g_norm.shape[0]):
        yp, ys = _layer(yp, ys, mem, seq, g_norm[l], g_mem[l], w_in[l], w_conv[l], a_log_f[l], a_log_b[l],
                        dt_bias_f[l], dt_bias_b[l], g_gdn_out[l], g_q_attn[l], g_k_attn[l], g_q_mem[l],
                        g_k_mem[l], w_mem_kv[l], w_down_a[l], w_down_b[l], w_down_c[l], w_out[l])
    return (yp.reshape(bp, seq, D_MODEL), ys.reshape(bs, seq, D_MODEL))
```
